```python
import jax, jax.numpy as jnp
from jax import lax
import numpy as np

D_MODEL = 1024
BATCH = 2
SEQ = 8192
DEPTH = 1

N_META = 16
EPS = 1e-6
DN_HEADS = 8
DN_DK = 128
DN_DV = 128
DN_CONV = 4
CHUNK = 64
QK_W = DN_HEADS * DN_DK
V_W = DN_HEADS * DN_DV
CF_CH = D_MODEL
CF_KERNEL = 31
IN_SIZES = (QK_W, QK_W, V_W, V_W, DN_HEADS, DN_HEADS, 2 * CF_CH, CF_CH, D_MODEL, D_MODEL)
IN_W = sum(IN_SIZES)

kernel_name = "hybrid_gdn_conformer_gated_merge"


def rmsnorm(x, w):
    xf = x.astype(jnp.float32)
    y = xf * lax.rsqrt(jnp.mean(xf * xf, axis=-1, keepdims=True) + EPS)
    return (y * w.astype(jnp.float32)).astype(x.dtype)


def layernorm(x, w, b):
    xf = x.astype(jnp.float32)
    mu = jnp.mean(xf, axis=-1, keepdims=True)
    var = jnp.mean(jnp.square(xf - mu), axis=-1, keepdims=True)
    y = (xf - mu) * lax.rsqrt(var + EPS)
    return (y * w.astype(jnp.float32) + b.astype(jnp.float32)).astype(x.dtype)


def l2norm(x):
    xf = x.astype(jnp.float32)
    return xf * lax.rsqrt(jnp.sum(xf * xf, axis=-1, keepdims=True) + EPS)


def causal_dwconv(x, w):
    k_w, ch = w.shape
    return lax.conv_general_dilated(
        x, w[:, None, :].astype(x.dtype), window_strides=(1,), padding=[(k_w - 1, 0)],
        dimension_numbers=("NWC", "WIO", "NWC"), feature_group_count=ch)


def chunk_gated_delta(q, k, v, beta, g):
    b, h, l, dk = q.shape
    dv = v.shape[-1]
    n = l // CHUNK
    q = q.reshape(b, h, n, CHUNK, dk)
    k = k.reshape(b, h, n, CHUNK, dk)
    v = v.reshape(b, h, n, CHUNK, dv)
    beta = beta.reshape(b, h, n, CHUNK)
    gc = jnp.cumsum(g.reshape(b, h, n, CHUNK), axis=-1)
    tri_incl = jnp.tril(jnp.ones((CHUNK, CHUNK), dtype=bool))
    tri_strict = jnp.tril(jnp.ones((CHUNK, CHUNK), dtype=bool), -1)
    decay = jnp.exp(jnp.where(tri_incl, gc[..., :, None] - gc[..., None, :], -jnp.inf))
    kk = jnp.einsum("bhncd,bhnsd->bhncs", k, k)
    lower = jnp.where(tri_strict, beta[..., :, None] * kk * decay, 0.0)
    a_mat = lower + jnp.eye(CHUNK, dtype=jnp.float32)
    rhs = jnp.concatenate([v * beta[..., None], k * (beta * jnp.exp(gc))[..., None]], axis=-1)
    sol = lax.linalg.triangular_solve(a_mat, rhs, left_side=True, lower=True, unit_diagonal=True)
    u, w = sol[..., :dv], sol[..., dv:]
    attn = jnp.einsum("bhncd,bhnsd->bhncs", q, k) * decay
    q_g = q * jnp.exp(gc)[..., None]
    g_last = gc[..., -1]
    k_end = k * jnp.exp(g_last[..., None] - gc)[..., None]

    def step(s, inp):
        qg_c, kend_c, u_c, w_c, at_c, gl_c = inp
        wv = u_c - jnp.einsum("bhck,bhkv->bhcv", w_c, s)
        o_c = jnp.einsum("bhck,bhkv->bhcv", qg_c, s) + jnp.einsum("bhcs,bhsv->bhcv", at_c, wv)
        s = s * jnp.exp(gl_c)[..., None, None] + jnp.einsum("bhck,bhcv->bhkv", kend_c, wv)
        return s, o_c

    xs = tuple(jnp.moveaxis(t, 2, 0) for t in (q_g, k_end, u, w, attn, g_last))
    s0 = jnp.zeros((b, h, dk, dv), jnp.float32)
    _, o = lax.scan(step, s0, xs)
    return jnp.moveaxis(o, 0, 2).reshape(b, h, l, dv)


def setup_inputs(seed: int = 0) -> dict:
    key = jax.random.key(seed)
    ks = jax.random.split(key, 24)
    f32 = jnp.float32
    nrm = lambda kk, shape, s: jax.random.normal(kk, shape, f32) * s
    x = nrm(ks[0], (BATCH, SEQ, D_MODEL), 1.0)
    meta = nrm(ks[1], (N_META, D_MODEL), 1.0)
    norm_w = 1.0 + nrm(ks[2], (DEPTH, D_MODEL), 0.02)
    w_in = nrm(ks[3], (DEPTH, D_MODEL, IN_W), D_MODEL ** -0.5)
    conv_qkv_w = nrm(ks[4], (DEPTH, DN_CONV, 2 * QK_W + V_W), DN_CONV ** -0.5)
    a_log = jnp.log(jax.random.uniform(ks[5], (DEPTH, DN_HEADS), f32, 1.0, 16.0))
    dt = jnp.exp(jax.random.uniform(ks[6], (DEPTH, DN_HEADS), f32, np.log(1e-3), np.log(1e-1)))
    dt_bias = dt + jnp.log(-jnp.expm1(-dt))
    dn_norm_w = 1.0 + nrm(ks[7], (DEPTH, DN_DV), 0.02)
    w_dn_out = nrm(ks[8], (DEPTH, V_W, D_MODEL), V_W ** -0.5)
    dw_w = nrm(ks[9], (DEPTH, CF_KERNEL, CF_CH), CF_KERNEL ** -0.5)
    dw_b = nrm(ks[10], (DEPTH, CF_CH), 0.01)
    ln_w = 1.0 + nrm(ks[11], (DEPTH, CF_CH), 0.02)
    ln_b = nrm(ks[12], (DEPTH, CF_CH), 0.01)
    w_cf_out = nrm(ks[13], (DEPTH, CF_CH, D_MODEL), CF_CH ** -0.5)
    b_cf_out = nrm(ks[14], (DEPTH, D_MODEL), 0.01)
    w_o = nrm(ks[15], (DEPTH, D_MODEL, D_MODEL), D_MODEL ** -0.5)
    final_norm_w = 1.0 + nrm(ks[16], (D_MODEL,), 0.02)
    return {"x": x, "meta": meta, "norm_w": norm_w, "w_in": w_in, "conv_qkv_w": conv_qkv_w,
            "a_log": a_log, "dt_bias": dt_bias, "dn_norm_w": dn_norm_w, "w_dn_out": w_dn_out,
            "dw_w": dw_w, "dw_b": dw_b, "ln_w": ln_w, "ln_b": ln_b, "w_cf_out": w_cf_out,
            "b_cf_out": b_cf_out, "w_o": w_o, "final_norm_w": final_norm_w}


def reference(x, meta, norm_w, w_in, conv_qkv_w, a_log, dt_bias, dn_norm_w, w_dn_out,
              dw_w, dw_b, ln_w, ln_b, w_cf_out, b_cf_out, w_o, final_norm_w):
    b = x.shape[0]
    x = jnp.concatenate([jnp.broadcast_to(meta[None].astype(x.dtype), (b, N_META, D_MODEL)), x], axis=1)
    l = x.shape[1]
    pad = (-l) % CHUNK
    split_idx = [int(v) for v in np.cumsum(IN_SIZES)[:-1]]
    for layer in range(DEPTH):
        h = rmsnorm(x, norm_w[layer])
        proj = h @ w_in[layer]
        q, k, v, za, b_lin, a_lin, glu, zb, ga, gb = jnp.split(proj, split_idx, axis=-1)

        qkv = jax.nn.silu(causal_dwconv(jnp.concatenate([q, k, v], axis=-1), conv_qkv_w[layer]))
        q, k, v = jnp.split(qkv, [QK_W, 2 * QK_W], axis=-1)
        q = l2norm(q.reshape(b, l, DN_HEADS, DN_DK)) * (DN_DK ** -0.5)
        k = l2norm(k.reshape(b, l, DN_HEADS, DN_DK))
        v = v.reshape(b, l, DN_HEADS, DN_DV).astype(jnp.float32)
        beta = jax.nn.sigmoid(b_lin.astype(jnp.float32))
        g = -jnp.exp(a_log[layer].astype(jnp.float32)) * jax.nn.softplus(
            a_lin.astype(jnp.float32) + dt_bias[layer].astype(jnp.float32))
        to_bh = lambda t: jnp.pad(jnp.moveaxis(t, 2, 1), [(0, 0), (0, 0), (pad, 0)] + [(0, 0)] * (t.ndim - 3))
        o = chunk_gated_delta(to_bh(q), to_bh(k), to_bh(v), to_bh(beta), to_bh(g))[:, :, pad:]
        o = jnp.moveaxis(o, 1, 2)
        o = rmsnorm(o, dn_norm_w[layer]) * jax.nn.silu(za.reshape(b, l, DN_HEADS, DN_DV).astype(jnp.float32))
        y_a = o.reshape(b, l, V_W).astype(x.dtype) @ w_dn_out[layer]

        glu_a, glu_b = jnp.split(glu, 2, axis=-1)
        c = glu_a * jax.nn.sigmoid(glu_b)
        c = causal_dwconv(c, dw_w[layer]) + dw_b[layer]
        c = jax.nn.silu(layernorm(c, ln_w[layer], ln_b[layer])) * jax.nn.silu(zb)
        y_b = c @ w_cf_out[layer] + b_cf_out[layer]

        merged = jax.nn.sigmoid(ga) * y_a + jax.nn.sigmoid(gb) * y_b
        x = x + merged @ w_o[layer]
    return rmsnorm(x, final_norm_w)[:, N_META:]
```

```python
import functools

import numpy as np
import jax
import jax.numpy as jnp
from jax import lax
from jax.experimental import pallas as pl
from jax.experimental.pallas import tpu as pltpu

D_MODEL = 1024
N_META = 16
EPS = 1e-6
DN_HEADS = 8
DN_DK = 128
DN_DV = 128
DN_CONV = 4
CHUNK = 64
CF_KERNEL = 31

TILE = 256
CHUNKS_PER_TILE = TILE // CHUNK
BLK = 1024
N_MAIN_BLK = 9
CONV_HALO = 8
DW_HALO = 32
VMEM_LIMIT = 56 * 1024 * 1024

F32 = jnp.float32
BF16 = jnp.bfloat16


def _mm(a, b):
    return jnp.dot(a.astype(BF16), b.astype(BF16), preferred_element_type=F32)


def _mm_nt(a, b):
    return lax.dot_general(a.astype(BF16), b.astype(BF16), (((1,), (1,)), ((), ())),
                           preferred_element_type=F32)


def _mm_tn(a, b):
    return lax.dot_general(a.astype(BF16), b.astype(BF16), (((0,), (0,)), ((), ())),
                           preferred_element_type=F32)


def _silu(x):
    return x * jax.nn.sigmoid(x)


def _proj_kernel(x_ref, meta_ref, nw_ref, w_ref, wba_ref, proj_ref, ba_ref):
    t = pl.program_id(1)
    xin = jnp.where(t == 0, meta_ref[...], x_ref[0])
    ms = jnp.mean(xin * xin, axis=-1, keepdims=True)
    h = (xin * lax.rsqrt(ms + EPS) * nw_ref[...]).astype(BF16)
    for j in range(N_MAIN_BLK):
        cs = slice(j * BLK, (j + 1) * BLK)
        proj_ref[0, :, cs] = jnp.dot(h, w_ref[:, cs], preferred_element_type=F32).astype(BF16)
    ba_ref[0] = jnp.dot(h, wba_ref[...], preferred_element_type=F32)


def _delta_kernel(q_ref, k_ref, v_ref, za_ref, ba_ref, cw_ref, alog_ref, dtb_ref, dnw_ref,
                  tri_ref, incl_ref, lvl_ref, og_ref, s_ref, cbuf_ref):
    t = pl.program_id(1)

    @pl.when(t == 0)
    def _():
        s_ref[...] = jnp.zeros_like(s_ref)
        cbuf_ref[0:CONV_HALO, :] = jnp.zeros((CONV_HALO, 3 * BLK), F32)

    cbuf_ref[CONV_HALO:CONV_HALO + TILE, 0:BLK] = q_ref[0].astype(F32)
    cbuf_ref[CONV_HALO:CONV_HALO + TILE, BLK:2 * BLK] = k_ref[0].astype(F32)
    cbuf_ref[CONV_HALO:CONV_HALO + TILE, 2 * BLK:3 * BLK] = v_ref[0].astype(F32)

    def conv_silu(col0):
        acc = None
        for j in range(DN_CONV):
            r0 = CONV_HALO - (DN_CONV - 1) + j
            term = cw_ref[j:j + 1, col0:col0 + DN_DK] * cbuf_ref[r0:r0 + TILE, col0:col0 + DN_DK]
            acc = term if acc is None else acc + term
        return _silu(acc)

    ba = ba_ref[0]
    beta_all = jax.nn.sigmoid(ba)
    g_all = -jnp.exp(alog_ref[...]) * jax.nn.softplus(ba + dtb_ref[...])
    csum = jnp.dot(tri_ref[...], g_all, preferred_element_type=F32,
                   precision=lax.Precision.HIGHEST)
    gc_all = csum[0:TILE]
    gl_all = csum[TILE:2 * TILE]
    gc_t = gc_all.T
    egc_all = jnp.exp(gc_all)
    eend_all = jnp.exp(gl_all - gc_all)
    egl_all = jnp.exp(gl_all)

    incl = incl_ref[0]
    strict = incl_ref[1]
    eye = incl - strict

    for h in range(DN_HEADS):
        c0 = h * DN_DK
        qs = conv_silu(c0)
        ks = conv_silu(BLK + c0)
        vs = conv_silu(2 * BLK + c0)
        qn = qs * lax.rsqrt(jnp.sum(qs * qs, axis=-1, keepdims=True) + EPS) * (DN_DK ** -0.5)
        kn = ks * lax.rsqrt(jnp.sum(ks * ks, axis=-1, keepdims=True) + EPS)

        c_beta = beta_all[:, h:h + 1]
        c_gc = gc_all[:, DN_HEADS + h:DN_HEADS + h + 1]
        r_gc = gc_t[DN_HEADS + h:DN_HEADS + h + 1, :]
        c_egc = egc_all[:, DN_HEADS + h:DN_HEADS + h + 1]
        c_eend = eend_all[:, DN_HEADS + h:DN_HEADS + h + 1]
        c_egl = egl_all[:, DN_HEADS + h:DN_HEADS + h + 1]

        decay = jnp.exp(jnp.where(incl > 0.5, c_gc - r_gc, -1e30))
        kk = _mm_nt(kn, kn)
        qk = _mm_nt(qn, kn)
        nmat = (c_beta * kk * decay) * strict
        attn = qk * decay

        tinv = eye - nmat * lvl_ref[0]
        for lv in range(1, 6):
            off = nmat * lvl_ref[lv]
            tinv = tinv - _mm(tinv, _mm(off, tinv))

        rhs = jnp.concatenate([vs * c_beta, kn * (c_beta * c_egc)], axis=1)
        sol = _mm(tinv, rhs)
        u = sol[:, 0:DN_DV]
        w = sol[:, DN_DV:DN_DV + DN_DK]
        qg = qn * c_egc
        kend = kn * c_eend

        s_h = s_ref[h]
        wv_parts, os_parts = [], []
        for c in range(CHUNKS_PER_TILE):
            rs = slice(c * CHUNK, (c + 1) * CHUNK)
            wv_c = u[rs] - _mm(w[rs], s_h)
            os_parts.append(_mm(qg[rs], s_h))
            s_h = s_h * c_egl[c * CHUNK:c * CHUNK + 1, :] + _mm_tn(kend[rs], wv_c)
            wv_parts.append(wv_c)
        s_ref[h] = s_h
        wv = jnp.concatenate(wv_parts, axis=0)
        o = jnp.concatenate(os_parts, axis=0) + _mm(attn, wv)

        on = o * lax.rsqrt(jnp.mean(o * o, axis=-1, keepdims=True) + EPS) * dnw_ref[...]
        za = za_ref[0, :, c0:c0 + DN_DV].astype(F32)
        og_ref[0, :, c0:c0 + DN_DV] = (on * _silu(za)).astype(BF16)

    cbuf_ref[0:CONV_HALO, :] = cbuf_ref[TILE:TILE + CONV_HALO, :]


def _merge_kernel(og_ref, glua_ref, glub_ref, zb_ref, ga_ref, gb_ref, x_ref,
                  wdn_ref, wcf_ref, wo_ref, dww_ref, dwb_ref, lnw_ref, lnb_ref, bcf_ref, fnw_ref,
                  out_ref, cbuf_ref, conv_ref):
    t = pl.program_id(1)

    @pl.when(t == 0)
    def _():
        cbuf_ref[0:DW_HALO, :] = jnp.zeros((DW_HALO, BLK), F32)

    cbuf_ref[DW_HALO:DW_HALO + TILE, :] = (glua_ref[0].astype(F32)
                                           * jax.nn.sigmoid(glub_ref[0].astype(F32)))

    rb = 64
    for cb in range(BLK // 128):
        cs = slice(cb * 128, (cb + 1) * 128)
        for r in range(TILE // rb):
            acc = jnp.broadcast_to(dwb_ref[:, cs], (rb, 128))
            for j in range(CF_KERNEL):
                r0 = DW_HALO - (CF_KERNEL - 1) + j + r * rb
                acc = acc + dww_ref[j:j + 1, cs] * cbuf_ref[r0:r0 + rb, cs]
            conv_ref[r * rb:(r + 1) * rb, cs] = acc

    cbuf_ref[0:DW_HALO, :] = cbuf_ref[TILE:TILE + DW_HALO, :]

    c = conv_ref[...]
    mu = jnp.mean(c, axis=-1, keepdims=True)
    cen = c - mu
    var = jnp.mean(cen * cen, axis=-1, keepdims=True)
    y = cen * lax.rsqrt(var + EPS) * lnw_ref[...] + lnb_ref[...]
    cc = _silu(y) * _silu(zb_ref[0].astype(F32))
    y_b = jnp.dot(cc.astype(BF16), wcf_ref[...], preferred_element_type=F32) + bcf_ref[...]
    y_a = jnp.dot(og_ref[0], wdn_ref[...], preferred_element_type=F32)
    merged = (jax.nn.sigmoid(ga_ref[0].astype(F32)) * y_a
              + jax.nn.sigmoid(gb_ref[0].astype(F32)) * y_b)
    xo = x_ref[0] + jnp.dot(merged.astype(BF16), wo_ref[...], preferred_element_type=F32)
    ms = jnp.mean(xo * xo, axis=-1, keepdims=True)
    out_ref[0] = xo * lax.rsqrt(ms + EPS) * fnw_ref[...]


def _tile_constants():
    i = np.arange(TILE)[:, None]
    j = np.arange(TILE)[None, :]
    same = (i // CHUNK) == (j // CHUNK)
    incl = (same & (i >= j)).astype(np.float32)
    strict = (same & (i > j)).astype(np.float32)
    tri = np.concatenate([incl, same.astype(np.float32)], axis=0)
    x = i ^ j
    lvls = []
    b = 1
    while b < CHUNK:
        lvls.append(((x >= b) & (x < 2 * b) & (i > j)).astype(np.float32))
        b *= 2
    return tri, np.stack([incl, strict]), np.stack(lvls)


def kernel(x, meta, norm_w, w_in, conv_qkv_w, a_log, dt_bias, dn_norm_w, w_dn_out,
           dw_w, dw_b, ln_w, ln_b, w_cf_out, b_cf_out, w_o, final_norm_w):
    bsz, seq, d = x.shape
    assert d == D_MODEL and seq % TILE == 0 and w_in.shape[0] == 1
    nt = seq // TILE + 1
    lp = nt * TILE

    w = w_in[0]
    qkvz = 4 * BLK
    w_main = jnp.concatenate([w[:, :qkvz], w[:, qkvz + 2 * DN_HEADS:]], axis=1).astype(BF16)
    w_ba = jnp.pad(w[:, qkvz:qkvz + 2 * DN_HEADS], ((0, 0), (0, 128 - 2 * DN_HEADS))).astype(BF16)
    meta_tile = jnp.pad(meta.astype(F32), ((TILE - N_META, 0), (0, 0)))
    row = lambda v: v.reshape(1, -1).astype(F32)
    alog_row = jnp.pad(row(a_log[0]), ((0, 0), (DN_HEADS, 128 - 2 * DN_HEADS)))
    dtb_row = jnp.pad(row(dt_bias[0]), ((0, 0), (DN_HEADS, 128 - 2 * DN_HEADS)))
    dww = jnp.pad(dw_w[0].astype(F32), ((0, 32 - CF_KERNEL), (0, 0)))
    tri, incl, lvls = _tile_constants()

    cparams = pltpu.CompilerParams(dimension_semantics=("arbitrary", "arbitrary"),
                                   vmem_limit_bytes=VMEM_LIMIT)
    xmap = lambda b, t: (b, jnp.maximum(t - 1, 0), 0)
    const2 = lambda b, t: (0, 0)
    const3 = lambda b, t: (0, 0, 0)

    proj, ba = pl.pallas_call(
        _proj_kernel,
        grid=(bsz, nt),
        in_specs=[
            pl.BlockSpec((1, TILE, D_MODEL), xmap),
            pl.BlockSpec((TILE, D_MODEL), const2),
            pl.BlockSpec((1, D_MODEL), const2),
            pl.BlockSpec((D_MODEL, N_MAIN_BLK * BLK), const2, pipeline_mode=pl.Buffered(1)),
            pl.BlockSpec((D_MODEL, 128), const2),
        ],
        out_specs=[
            pl.BlockSpec((1, TILE, N_MAIN_BLK * BLK), lambda b, t: (b, t, 0)),
            pl.BlockSpec((1, TILE, 128), lambda b, t: (b, t, 0)),
        ],
        out_shape=[
            jax.ShapeDtypeStruct((bsz, lp, N_MAIN_BLK * BLK), BF16),
            jax.ShapeDtypeStruct((bsz, lp, 128), F32),
        ],
        compiler_params=cparams,
        name="proj",
    )(x, meta_tile, row(norm_w[0]), w_main, w_ba)

    col = lambda j: pl.BlockSpec((1, TILE, BLK), lambda b, t, j=j: (b, t, j))
    og = pl.pallas_call(
        _delta_kernel,
        grid=(bsz, nt),
        in_specs=[
            col(0), col(1), col(2), col(3),
            pl.BlockSpec((1, TILE, 128), lambda b, t: (b, t, 0)),
            pl.BlockSpec((DN_CONV, 3 * BLK), const2),
            pl.BlockSpec((1, 128), const2),
            pl.BlockSpec((1, 128), const2),
            pl.BlockSpec((1, DN_DV), const2),
            pl.BlockSpec((2 * TILE, TILE), const2),
            pl.BlockSpec((2, TILE, TILE), const3),
            pl.BlockSpec((6, TILE, TILE), const3),
        ],
        out_specs=pl.BlockSpec((1, TILE, BLK), lambda b, t: (b, t, 0)),
        out_shape=jax.ShapeDtypeStruct((bsz, lp, BLK), BF16),
        scratch_shapes=[
            pltpu.VMEM((DN_HEADS, DN_DK, DN_DV), F32),
            pltpu.VMEM((CONV_HALO + TILE, 3 * BLK), F32),
        ],
        compiler_params=cparams,
        name="delta",
    )(proj, proj, proj, proj, ba, conv_qkv_w[0].astype(F32), alog_row, dtb_row,
      row(dn_norm_w[0]), jnp.asarray(tri), jnp.asarray(incl), jnp.asarray(lvls))

    wspec = pl.BlockSpec((D_MODEL, D_MODEL), const2)
    vspec = pl.BlockSpec((1, D_MODEL), const2)
    out = pl.pallas_call(
        _merge_kernel,
        grid=(bsz, nt),
        in_specs=[
            pl.BlockSpec((1, TILE, BLK), lambda b, t: (b, t, 0)),
            col(4), col(5), col(6), col(7), col(8),
            pl.BlockSpec((1, TILE, D_MODEL), xmap),
            wspec, wspec, wspec,
            pl.BlockSpec((32, D_MODEL), const2),
            vspec, vspec, vspec, vspec, vspec,
        ],
        out_specs=pl.BlockSpec((1, TILE, D_MODEL), xmap),
        out_shape=jax.ShapeDtypeStruct((bsz, seq, D_MODEL), F32),
        scratch_shapes=[
            pltpu.VMEM((DW_HALO + TILE, BLK), F32),
            pltpu.VMEM((TILE, BLK), F32),
        ],
        compiler_params=cparams,
        name="merge",
    )(og, proj, proj, proj, proj, proj, x,
      w_dn_out[0].astype(BF16), w_cf_out[0].astype(BF16), w_o[0].astype(BF16),
      dww, row(dw_b[0]), row(ln_w[0]), row(ln_b[0]), row(b_cf_out[0]), row(final_norm_w))
    return out
```

```python
import functools

import numpy as np
import jax
import jax.numpy as jnp
from jax import lax
from jax.experimental import pallas as pl
from jax.experimental.pallas import tpu as pltpu

D_MODEL = 1024
N_META = 16
EPS = 1e-6
DN_HEADS = 8
DN_DK = 128
DN_DV = 128
DN_CONV = 4
CHUNK = 64
CF_KERNEL = 31

TILE = 256
CHUNKS_PER_TILE = TILE // CHUNK
BLK = 1024
N_MAIN_BLK = 9
CONV_HALO = 8
DW_HALO = 32
VMEM_LIMIT = 56 * 1024 * 1024

F32 = jnp.float32
BF16 = jnp.bfloat16


def _mm(a, b):
    return jnp.dot(a.astype(BF16), b.astype(BF16), preferred_element_type=F32)


def _mm_nt(a, b):
    return lax.dot_general(a.astype(BF16), b.astype(BF16), (((1,), (1,)), ((), ())),
                           preferred_element_type=F32)


def _mm_tn(a, b):
    return lax.dot_general(a.astype(BF16), b.astype(BF16), (((0,), (0,)), ((), ())),
                           preferred_element_type=F32)


def _silu(x):
    return x * jax.nn.sigmoid(x)


def _proj_kernel(x_ref, meta_ref, nw_ref, w_ref, wba_ref, proj_ref, ba_ref):
    t = pl.program_id(1)
    xin = jnp.where(t == 0, meta_ref[...], x_ref[0])
    ms = jnp.mean(xin * xin, axis=-1, keepdims=True)
    h = (xin * lax.rsqrt(ms + EPS) * nw_ref[...]).astype(BF16)
    for j in range(N_MAIN_BLK):
        cs = slice(j * BLK, (j + 1) * BLK)
        proj_ref[0, :, cs] = jnp.dot(h, w_ref[:, cs], preferred_element_type=F32).astype(BF16)
    ba_ref[0] = jnp.dot(h, wba_ref[...], preferred_element_type=F32)


def _delta_kernel(q_ref, k_ref, v_ref, za_ref, ba_ref, cw_ref, alog_ref, dtb_ref, dnw_ref,
                  tri_ref, incl_ref, lvl_ref, og_ref, s_ref, cbuf_ref):
    t = pl.program_id(1)

    @pl.when(t == 0)
    def _():
        s_ref[...] = jnp.zeros_like(s_ref)
        cbuf_ref[0:CONV_HALO, :] = jnp.zeros((CONV_HALO, 3 * BLK), F32)

    cbuf_ref[CONV_HALO:CONV_HALO + TILE, 0:BLK] = q_ref[0].astype(F32)
    cbuf_ref[CONV_HALO:CONV_HALO + TILE, BLK:2 * BLK] = k_ref[0].astype(F32)
    cbuf_ref[CONV_HALO:CONV_HALO + TILE, 2 * BLK:3 * BLK] = v_ref[0].astype(F32)

    def conv_silu(col0):
        acc = None
        for j in range(DN_CONV):
            r0 = CONV_HALO - (DN_CONV - 1) + j
            term = cw_ref[j:j + 1, col0:col0 + DN_DK] * cbuf_ref[r0:r0 + TILE, col0:col0 + DN_DK]
            acc = term if acc is None else acc + term
        return _silu(acc)

    ba = ba_ref[0]
    beta_all = jax.nn.sigmoid(ba)
    g_all = -jnp.exp(alog_ref[...]) * jax.nn.softplus(ba + dtb_ref[...])
    csum = jnp.dot(tri_ref[...], g_all, preferred_element_type=F32,
                   precision=lax.Precision.HIGHEST)
    gc_all = csum[0:TILE]
    gl_all = csum[TILE:2 * TILE]
    gc_t = gc_all.T
    egc_all = jnp.exp(gc_all)
    eend_all = jnp.exp(gl_all - gc_all)
    egl_all = jnp.exp(gl_all)

    incl = incl_ref[0]
    strict = incl_ref[1]
    eye_bf = (incl - strict).astype(BF16)
    heads = range(DN_HEADS)

    nm, tv, attn, rhs, qg, kend, egl = [], [], [], [], [], [], []
    for h in heads:
        c0 = h * DN_DK
        qs = conv_silu(c0)
        ks = conv_silu(BLK + c0)
        vs = conv_silu(2 * BLK + c0)
        qn = qs * lax.rsqrt(jnp.sum(qs * qs, axis=-1, keepdims=True) + EPS) * (DN_DK ** -0.5)
        kn = ks * lax.rsqrt(jnp.sum(ks * ks, axis=-1, keepdims=True) + EPS)

        c_beta = beta_all[:, h:h + 1]
        c_gc = gc_all[:, DN_HEADS + h:DN_HEADS + h + 1]
        r_gc = gc_t[DN_HEADS + h:DN_HEADS + h + 1, :]
        c_egc = egc_all[:, DN_HEADS + h:DN_HEADS + h + 1]
        c_eend = eend_all[:, DN_HEADS + h:DN_HEADS + h + 1]
        egl.append(egl_all[:, DN_HEADS + h:DN_HEADS + h + 1])

        kb = kn * c_beta
        decay = jnp.exp(jnp.where(incl > 0.5, c_gc - r_gc, -1e30))
        nm_h = (_mm_nt(kb, kn) * decay * strict).astype(BF16)
        nm.append(nm_h)
        attn.append((_mm_nt(qn, kn) * decay).astype(BF16))
        rhs.append(jnp.concatenate([vs * c_beta, kb * c_egc], axis=1).astype(BF16))
        qg.append((qn * c_egc).astype(BF16))
        kend.append((kn * c_eend).astype(BF16))
        tv.append(eye_bf - nm_h * lvl_ref[0])

    for lv in range(1, 6):
        x1 = [jnp.dot(nm[h] * lvl_ref[lv], tv[h], preferred_element_type=F32).astype(BF16)
              for h in heads]
        tv = [tv[h] - jnp.dot(tv[h], x1[h], preferred_element_type=F32).astype(BF16)
              for h in heads]

    u, w = [], []
    for h in heads:
        sol = jnp.dot(tv[h], rhs[h], preferred_element_type=F32)
        u.append(sol[:, 0:DN_DV])
        w.append(sol[:, DN_DV:DN_DV + DN_DK].astype(BF16))

    s = [s_ref[h] for h in heads]
    wv = [[] for _ in heads]
    o_s = [[] for _ in heads]
    for c in range(CHUNKS_PER_TILE):
        rs = slice(c * CHUNK, (c + 1) * CHUNK)
        for h in heads:
            s_bf = s[h].astype(BF16)
            ws = jnp.dot(jnp.concatenate([w[h][rs], qg[h][rs]], axis=0), s_bf,
                         preferred_element_type=F32)
            wv_c = u[h][rs] - ws[0:CHUNK]
            o_s[h].append(ws[CHUNK:2 * CHUNK])
            s[h] = s[h] * egl[h][c * CHUNK:c * CHUNK + 1, :] + _mm_tn(kend[h][rs], wv_c)
            wv[h].append(wv_c.astype(BF16))

    for h in heads:
        c0 = h * DN_DK
        s_ref[h] = s[h]
        o = jnp.concatenate(o_s[h], axis=0) + jnp.dot(attn[h], jnp.concatenate(wv[h], axis=0),
                                                      preferred_element_type=F32)
        on = o * lax.rsqrt(jnp.mean(o * o, axis=-1, keepdims=True) + EPS) * dnw_ref[...]
        za = za_ref[0, :, c0:c0 + DN_DV].astype(F32)
        og_ref[0, :, c0:c0 + DN_DV] = (on * _silu(za)).astype(BF16)

    cbuf_ref[0:CONV_HALO, :] = cbuf_ref[TILE:TILE + CONV_HALO, :]


def _merge_kernel(og_ref, glua_ref, glub_ref, zb_ref, ga_ref, gb_ref, x_ref,
                  wdn_ref, wcf_ref, wo_ref, dww_ref, dwb_ref, lnw_ref, lnb_ref, bcf_ref, fnw_ref,
                  out_ref, cbuf_ref, conv_ref):
    t = pl.program_id(1)

    @pl.when(t == 0)
    def _():
        cbuf_ref[0:DW_HALO, :] = jnp.zeros((DW_HALO, BLK), F32)

    cbuf_ref[DW_HALO:DW_HALO + TILE, :] = (glua_ref[0].astype(F32)
                                           * jax.nn.sigmoid(glub_ref[0].astype(F32)))

    rb = 64
    for cb in range(BLK // 128):
        cs = slice(cb * 128, (cb + 1) * 128)
        for r in range(TILE // rb):
            acc = jnp.broadcast_to(dwb_ref[:, cs], (rb, 128))
            for j in range(CF_KERNEL):
                r0 = DW_HALO - (CF_KERNEL - 1) + j + r * rb
                acc = acc + dww_ref[j:j + 1, cs] * cbuf_ref[r0:r0 + rb, cs]
            conv_ref[r * rb:(r + 1) * rb, cs] = acc

    cbuf_ref[0:DW_HALO, :] = cbuf_ref[TILE:TILE + DW_HALO, :]

    c = conv_ref[...]
    mu = jnp.mean(c, axis=-1, keepdims=True)
    cen = c - mu
    var = jnp.mean(cen * cen, axis=-1, keepdims=True)
    y = cen * lax.rsqrt(var + EPS) * lnw_ref[...] + lnb_ref[...]
    cc = _silu(y) * _silu(zb_ref[0].astype(F32))
    y_b = jnp.dot(cc.astype(BF16), wcf_ref[...], preferred_element_type=F32) + bcf_ref[...]
    y_a = jnp.dot(og_ref[0], wdn_ref[...], preferred_element_type=F32)
    merged = (jax.nn.sigmoid(ga_ref[0].astype(F32)) * y_a
              + jax.nn.sigmoid(gb_ref[0].astype(F32)) * y_b)
    xo = x_ref[0] + jnp.dot(merged.astype(BF16), wo_ref[...], preferred_element_type=F32)
    ms = jnp.mean(xo * xo, axis=-1, keepdims=True)
    out_ref[0] = xo * lax.rsqrt(ms + EPS) * fnw_ref[...]


def _tile_constants():
    i = np.arange(TILE)[:, None]
    j = np.arange(TILE)[None, :]
    same = (i // CHUNK) == (j // CHUNK)
    incl = (same & (i >= j)).astype(np.float32)
    strict = (same & (i > j)).astype(np.float32)
    tri = np.concatenate([incl, same.astype(np.float32)], axis=0)
    x = i ^ j
    lvls = []
    b = 1
    while b < CHUNK:
        lvls.append(((x >= b) & (x < 2 * b) & (i > j)).astype(np.float32))
        b *= 2
    return tri, np.stack([incl, strict]), np.stack(lvls)


def kernel(x, meta, norm_w, w_in, conv_qkv_w, a_log, dt_bias, dn_norm_w, w_dn_out,
           dw_w, dw_b, ln_w, ln_b, w_cf_out, b_cf_out, w_o, final_norm_w):
    bsz, seq, d = x.shape
    assert d == D_MODEL and seq % TILE == 0 and w_in.shape[0] == 1
    nt = seq // TILE + 1
    lp = nt * TILE

    w = w_in[0]
    qkvz = 4 * BLK
    w_main = jnp.concatenate([w[:, :qkvz], w[:, qkvz + 2 * DN_HEADS:]], axis=1).astype(BF16)
    w_ba = jnp.pad(w[:, qkvz:qkvz + 2 * DN_HEADS], ((0, 0), (0, 128 - 2 * DN_HEADS))).astype(BF16)
    meta_tile = jnp.pad(meta.astype(F32), ((TILE - N_META, 0), (0, 0)))
    row = lambda v: v.reshape(1, -1).astype(F32)
    alog_row = jnp.pad(row(a_log[0]), ((0, 0), (DN_HEADS, 128 - 2 * DN_HEADS)))
    dtb_row = jnp.pad(row(dt_bias[0]), ((0, 0), (DN_HEADS, 128 - 2 * DN_HEADS)))
    dww = jnp.pad(dw_w[0].astype(F32), ((0, 32 - CF_KERNEL), (0, 0)))
    tri, incl, lvls = _tile_constants()

    cparams = pltpu.CompilerParams(dimension_semantics=("arbitrary", "arbitrary"),
                                   vmem_limit_bytes=VMEM_LIMIT)
    xmap = lambda b, t: (b, jnp.maximum(t - 1, 0), 0)
    const2 = lambda b, t: (0, 0)
    const3 = lambda b, t: (0, 0, 0)

    proj, ba = pl.pallas_call(
        _proj_kernel,
        grid=(bsz, nt),
        in_specs=[
            pl.BlockSpec((1, TILE, D_MODEL), xmap),
            pl.BlockSpec((TILE, D_MODEL), const2),
            pl.BlockSpec((1, D_MODEL), const2),
            pl.BlockSpec((D_MODEL, N_MAIN_BLK * BLK), const2, pipeline_mode=pl.Buffered(1)),
            pl.BlockSpec((D_MODEL, 128), const2),
        ],
        out_specs=[
            pl.BlockSpec((1, TILE, N_MAIN_BLK * BLK), lambda b, t: (b, t, 0)),
            pl.BlockSpec((1, TILE, 128), lambda b, t: (b, t, 0)),
        ],
        out_shape=[
            jax.ShapeDtypeStruct((bsz, lp, N_MAIN_BLK * BLK), BF16),
            jax.ShapeDtypeStruct((bsz, lp, 128), F32),
        ],
        compiler_params=cparams,
        name="proj",
    )(x, meta_tile, row(norm_w[0]), w_main, w_ba)

    col = lambda j: pl.BlockSpec((1, TILE, BLK), lambda b, t, j=j: (b, t, j))
    og = pl.pallas_call(
        _delta_kernel,
        grid=(bsz, nt),
        in_specs=[
            col(0), col(1), col(2), col(3),
            pl.BlockSpec((1, TILE, 128), lambda b, t: (b, t, 0)),
            pl.BlockSpec((DN_CONV, 3 * BLK), const2),
            pl.BlockSpec((1, 128), const2),
            pl.BlockSpec((1, 128), const2),
            pl.BlockSpec((1, DN_DV), const2),
            pl.BlockSpec((2 * TILE, TILE), const2),
            pl.BlockSpec((2, TILE, TILE), const3),
            pl.BlockSpec((6, TILE, TILE), const3),
        ],
        out_specs=pl.BlockSpec((1, TILE, BLK), lambda b, t: (b, t, 0)),
        out_shape=jax.ShapeDtypeStruct((bsz, lp, BLK), BF16),
        scratch_shapes=[
            pltpu.VMEM((DN_HEADS, DN_DK, DN_DV), F32),
            pltpu.VMEM((CONV_HALO + TILE, 3 * BLK), F32),
        ],
        compiler_params=cparams,
        name="delta",
    )(proj, proj, proj, proj, ba, conv_qkv_w[0].astype(F32), alog_row, dtb_row,
      row(dn_norm_w[0]), jnp.asarray(tri), jnp.asarray(incl), jnp.asarray(lvls, BF16))

    wspec = pl.BlockSpec((D_MODEL, D_MODEL), const2)
    vspec = pl.BlockSpec((1, D_MODEL), const2)
    out = pl.pallas_call(
        _merge_kernel,
        grid=(bsz, nt),
        in_specs=[
            pl.BlockSpec((1, TILE, BLK), lambda b, t: (b, t, 0)),
            col(4), col(5), col(6), col(7), col(8),
            pl.BlockSpec((1, TILE, D_MODEL), xmap),
            wspec, wspec, wspec,
            pl.BlockSpec((32, D_MODEL), const2),
            vspec, vspec, vspec, vspec, vspec,
        ],
        out_specs=pl.BlockSpec((1, TILE, D_MODEL), xmap),
        out_shape=jax.ShapeDtypeStruct((bsz, seq, D_MODEL), F32),
        scratch_shapes=[
            pltpu.VMEM((DW_HALO + TILE, BLK), F32),
            pltpu.VMEM((TILE, BLK), F32),
        ],
        compiler_params=cparams,
        name="merge",
    )(og, proj, proj, proj, proj, proj, x,
      w_dn_out[0].astype(BF16), w_cf_out[0].astype(BF16), w_o[0].astype(BF16),
      dww, row(dw_b[0]), row(ln_w[0]), row(ln_b[0]), row(b_cf_out[0]), row(final_norm_w))
    return out
```

```python
import numpy as np
import jax
import jax.numpy as jnp
from jax import lax
from jax.experimental import pallas as pl
from jax.experimental.pallas import tpu as pltpu

D_MODEL = 1024
N_META = 16
EPS = 1e-6
DN_HEADS = 8
DN_DK = 128
DN_DV = 128
DN_CONV = 4
CHUNK = 64
CF_KERNEL = 31

TILE = 256
HALF = TILE // 2
CHUNKS_PER_TILE = TILE // CHUNK
BLK = 1024
N_MAIN_BLK = 9
DW_HALO = 32
SH_ROWS = DW_HALO + TILE - 8
VMEM_LIMIT = 56 * 1024 * 1024

F32 = jnp.float32
BF16 = jnp.bfloat16
U32 = jnp.uint32


def _unpack(ref_slice):
    return pltpu.bitcast(ref_slice, BF16)


def _pack(x):
    return pltpu.bitcast(x.astype(BF16), U32)


def _pack_rows(a):
    *lead, m, n = a.shape
    return lax.bitcast_convert_type(jnp.swapaxes(a.reshape(*lead, m // 2, 2, n), -1, -2), U32)


def _mm_nt(a, b):
    return lax.dot_general(a.astype(BF16), b.astype(BF16), (((1,), (1,)), ((), ())),
                           preferred_element_type=F32)


def _mm_tn(a, b):
    return lax.dot_general(a.astype(BF16), b.astype(BF16), (((0,), (0,)), ((), ())),
                           preferred_element_type=F32)


def _silu(x):
    return x * jax.nn.sigmoid(x)


def _proj_kernel(x_ref, meta_ref, nw_ref, w_ref, wba_ref, cw_ref, proj_ref, ba_ref, pre_ref, tap_ref):
    t = pl.program_id(1)
    last = DN_CONV - 1

    @pl.when(t == 0)
    def _():
        pre_ref[TILE:TILE + 8, :] = jnp.zeros((8, 3 * BLK), F32)

    xin = jnp.where(t == 0, meta_ref[...], x_ref[0])
    ms = jnp.mean(xin * xin, axis=-1, keepdims=True)
    h = (xin * lax.rsqrt(ms + EPS) * nw_ref[...]).astype(BF16)

    pre_ref[0:8, :] = pre_ref[TILE:TILE + 8, :]
    for j in range(3):
        cs = slice(j * BLK, (j + 1) * BLK)
        pre_ref[8:8 + TILE, cs] = jnp.dot(h, _unpack(w_ref[:, cs]), preferred_element_type=F32)
        for tap in range(last):
            d = last - tap
            tap_ref[tap, :, cs] = pre_ref[8 - d:8 - d + TILE, cs]

    def conv_block(kind, hh):
        col0 = kind * BLK + hh * DN_DK
        acc = cw_ref[last:last + 1, col0:col0 + DN_DK] * pre_ref[8:8 + TILE, col0:col0 + DN_DK]
        for tap in range(last):
            acc = acc + cw_ref[tap:tap + 1, col0:col0 + DN_DK] * tap_ref[tap, :, col0:col0 + DN_DK]
        y = _silu(acc)
        if kind == 0:
            y = y * lax.rsqrt(jnp.sum(y * y, axis=-1, keepdims=True) + EPS) * (DN_DK ** -0.5)
        elif kind == 1:
            y = y * lax.rsqrt(jnp.sum(y * y, axis=-1, keepdims=True) + EPS)
        proj_ref[0, :, col0:col0 + DN_DK] = _pack(y)

    blocks = [(kind, hh) for kind in range(3) for hh in range(DN_HEADS)]
    sub = (N_MAIN_BLK - 3) * BLK // len(blocks)
    for i, (kind, hh) in enumerate(blocks):
        cs = slice(3 * BLK + i * sub, 3 * BLK + (i + 1) * sub)
        proj_ref[0, :, cs] = _pack(jnp.dot(h, _unpack(w_ref[:, cs]), preferred_element_type=F32))
        conv_block(kind, hh)
    ba_ref[0] = jnp.dot(h, _unpack(wba_ref[...]), preferred_element_type=F32)


def _delta_kernel(q_ref, k_ref, v_ref, za_ref, ba_ref, alog_ref, dtb_ref, dnw_ref,
                  tri_ref, msk_ref, lvl_ref, og_ref, s_ref):
    t = pl.program_id(1)

    @pl.when(t == 0)
    def _():
        s_ref[...] = jnp.zeros_like(s_ref)

    ba = ba_ref[0]
    beta_all = jax.nn.sigmoid(ba)
    g_all = -jnp.exp(alog_ref[...]) * jax.nn.softplus(ba + dtb_ref[...])
    csum = jnp.dot(tri_ref[...], g_all, preferred_element_type=F32,
                   precision=lax.Precision.HIGHEST)
    gc_all = csum[0:TILE]
    gl_all = csum[TILE:2 * TILE]
    gc_t = gc_all.T
    egc_all = jnp.exp(gc_all)
    eend_all = jnp.exp(gl_all - gc_all)
    egl_all = jnp.exp(gl_all)

    neg = msk_ref[0]
    eye_bf = msk_ref[1].astype(BF16)
    heads = range(DN_HEADS)

    nm, tv, attn, rhs, qg, kend, egl = [], [], [], [], [], [], []
    for h in heads:
        c0 = h * DN_DK
        qn = _unpack(q_ref[0, :, c0:c0 + DN_DK]).astype(F32)
        kn_bf = _unpack(k_ref[0, :, c0:c0 + DN_DK])
        kn = kn_bf.astype(F32)
        vs = _unpack(v_ref[0, :, c0:c0 + DN_DV]).astype(F32)

        c_beta = beta_all[:, h:h + 1]
        c_gc = gc_all[:, DN_HEADS + h:DN_HEADS + h + 1]
        r_gc = gc_t[DN_HEADS + h:DN_HEADS + h + 1, :]
        c_egc = egc_all[:, DN_HEADS + h:DN_HEADS + h + 1]
        c_eend = eend_all[:, DN_HEADS + h:DN_HEADS + h + 1]
        egl.append(egl_all[:, DN_HEADS + h:DN_HEADS + h + 1])

        kb = kn * c_beta
        decay = jnp.exp((c_gc - r_gc) + neg)
        nm_h = (_mm_nt(kb, kn_bf) * decay).astype(BF16)
        nm.append(nm_h)
        attn.append((_mm_nt(qn, kn_bf) * decay).astype(BF16))
        rhs.append(jnp.concatenate([vs * c_beta, kb * c_egc], axis=1).astype(BF16))
        qg.append((qn * c_egc).astype(BF16))
        kend.append((kn * c_eend).astype(BF16))
        tv.append(eye_bf - nm_h * lvl_ref[0])

    for lv in range(1, 6):
        x1 = [jnp.dot(nm[h] * lvl_ref[lv], tv[h], preferred_element_type=F32).astype(BF16)
              for h in heads]
        tv = [tv[h] - jnp.dot(tv[h], x1[h], preferred_element_type=F32).astype(BF16)
              for h in heads]

    u, w = [], []
    for h in heads:
        sol = jnp.dot(tv[h], rhs[h], preferred_element_type=F32)
        u.append(sol[:, 0:DN_DV])
        w.append(sol[:, DN_DV:DN_DV + DN_DK].astype(BF16))

    s = [s_ref[h] for h in heads]
    wv = [[] for _ in heads]
    o_s = [[] for _ in heads]
    for c in range(CHUNKS_PER_TILE):
        rs = slice(c * CHUNK, (c + 1) * CHUNK)
        for h in heads:
            s_bf = s[h].astype(BF16)
            ws = jnp.dot(jnp.concatenate([w[h][rs], qg[h][rs]], axis=0), s_bf,
                         preferred_element_type=F32)
            wv_c = u[h][rs] - ws[0:CHUNK]
            o_s[h].append(ws[CHUNK:2 * CHUNK])
            s[h] = s[h] * egl[h][c * CHUNK:c * CHUNK + 1, :] + _mm_tn(kend[h][rs], wv_c)
            wv[h].append(wv_c.astype(BF16))

    for h in heads:
        c0 = h * DN_DK
        s_ref[h] = s[h]
        o = jnp.concatenate(o_s[h], axis=0) + jnp.dot(attn[h], jnp.concatenate(wv[h], axis=0),
                                                      preferred_element_type=F32)
        on = o * lax.rsqrt(jnp.mean(o * o, axis=-1, keepdims=True) + EPS) * dnw_ref[...]
        za = _unpack(za_ref[0, :, c0:c0 + DN_DV]).astype(F32)
        og_ref[0, :, c0:c0 + DN_DV] = _pack(on * _silu(za))


def _merge_kernel(og_ref, glua_ref, glub_ref, zb_ref, ga_ref, gb_ref, x_ref,
                  wdn_ref, wcf_ref, wo_ref, dww_ref, dwb_ref, lnw_ref, lnb_ref, bcf_ref, fnw_ref,
                  out_ref, cbuf_ref, conv_ref, sh_ref):
    t = pl.program_id(1)

    @pl.when(t == 0)
    def _():
        cbuf_ref[0:DW_HALO, :] = jnp.zeros((DW_HALO, BLK), F32)

    cbuf_ref[DW_HALO:DW_HALO + TILE, :] = (_unpack(glua_ref[0]).astype(F32)
                                           * jax.nn.sigmoid(_unpack(glub_ref[0]).astype(F32)))

    rb = 64
    first = DW_HALO - (CF_KERNEL - 1)
    for cb in range(BLK // 128):
        cs = slice(cb * 128, (cb + 1) * 128)
        slot = cb % 2
        for s in range(1, 8):
            sh_ref[slot, s - 1] = cbuf_ref[s:s + SH_ROWS, cs]
        for r in range(TILE // rb):
            acc = jnp.broadcast_to(dwb_ref[:, cs], (rb, 128))
            for j in range(CF_KERNEL):
                s = (first + j) % 8
                r0 = (first + j) - s + r * rb
                if s == 0:
                    rows = cbuf_ref[r0:r0 + rb, cs]
                else:
                    rows = sh_ref[slot, s - 1, r0:r0 + rb, :]
                acc = acc + dww_ref[j:j + 1, cs] * rows
            conv_ref[r * rb:(r + 1) * rb, cs] = acc

    cbuf_ref[0:DW_HALO, :] = cbuf_ref[TILE:TILE + DW_HALO, :]

    c = conv_ref[...]
    mu = jnp.mean(c, axis=-1, keepdims=True)
    cen = c - mu
    var = jnp.mean(cen * cen, axis=-1, keepdims=True)
    y = cen * lax.rsqrt(var + EPS) * lnw_ref[...] + lnb_ref[...]
    cc = _silu(y) * _silu(_unpack(zb_ref[0]).astype(F32))
    y_b = jnp.dot(cc.astype(BF16), _unpack(wcf_ref[...]), preferred_element_type=F32) + bcf_ref[...]
    y_a = jnp.dot(_unpack(og_ref[0]), _unpack(wdn_ref[...]), preferred_element_type=F32)
    merged = (jax.nn.sigmoid(_unpack(ga_ref[0]).astype(F32)) * y_a
              + jax.nn.sigmoid(_unpack(gb_ref[0]).astype(F32)) * y_b)
    xo = x_ref[0] + jnp.dot(merged.astype(BF16), _unpack(wo_ref[...]), preferred_element_type=F32)
    ms = jnp.mean(xo * xo, axis=-1, keepdims=True)
    out_ref[0] = xo * lax.rsqrt(ms + EPS) * fnw_ref[...]


def _tile_constants():
    i = np.arange(TILE)[:, None]
    j = np.arange(TILE)[None, :]
    same = (i // CHUNK) == (j // CHUNK)
    incl = (same & (i >= j)).astype(np.float32)
    tri = np.concatenate([incl, same.astype(np.float32)], axis=0)
    x = i ^ j
    lvls = []
    b = 1
    while b < CHUNK:
        lvls.append(((x >= b) & (x < 2 * b) & (i > j)).astype(np.float32))
        b *= 2
    neg = np.where(incl > 0, 0.0, -1e30).astype(np.float32)
    return tri, np.stack([neg, np.eye(TILE, dtype=np.float32)]), np.stack(lvls)


def kernel(x, meta, norm_w, w_in, conv_qkv_w, a_log, dt_bias, dn_norm_w, w_dn_out,
           dw_w, dw_b, ln_w, ln_b, w_cf_out, b_cf_out, w_o, final_norm_w):
    bsz, seq, d = x.shape
    assert d == D_MODEL and seq % TILE == 0 and w_in.shape[0] == 1
    nt = seq // TILE + 1
    lp = nt * TILE

    w = w_in[0]
    qkvz = 4 * BLK
    w_main = _pack_rows(jnp.concatenate([w[:, :qkvz], w[:, qkvz + 2 * DN_HEADS:]], axis=1).astype(BF16))
    w_ba = _pack_rows(jnp.pad(w[:, qkvz:qkvz + 2 * DN_HEADS],
                              ((0, 0), (0, 128 - 2 * DN_HEADS))).astype(BF16))
    meta_tile = jnp.pad(meta.astype(F32), ((TILE - N_META, 0), (0, 0)))
    row = lambda v: v.reshape(1, -1).astype(F32)
    alog_row = jnp.pad(row(a_log[0]), ((0, 0), (DN_HEADS, 128 - 2 * DN_HEADS)))
    dtb_row = jnp.pad(row(dt_bias[0]), ((0, 0), (DN_HEADS, 128 - 2 * DN_HEADS)))
    dww = jnp.pad(dw_w[0].astype(F32), ((0, 32 - CF_KERNEL), (0, 0)))
    tri, msk, lvls = _tile_constants()

    cparams = pltpu.CompilerParams(dimension_semantics=("arbitrary", "arbitrary"),
                                   vmem_limit_bytes=VMEM_LIMIT)
    xmap = lambda b, t: (b, jnp.maximum(t - 1, 0), 0)
    const2 = lambda b, t: (0, 0)
    const3 = lambda b, t: (0, 0, 0)

    proj, ba = pl.pallas_call(
        _proj_kernel,
        grid=(bsz, nt),
        in_specs=[
            pl.BlockSpec((1, TILE, D_MODEL), xmap),
            pl.BlockSpec((TILE, D_MODEL), const2),
            pl.BlockSpec((1, D_MODEL), const2),
            pl.BlockSpec((D_MODEL // 2, N_MAIN_BLK * BLK), const2, pipeline_mode=pl.Buffered(1)),
            pl.BlockSpec((D_MODEL // 2, 128), const2),
            pl.BlockSpec((DN_CONV, 3 * BLK), const2),
        ],
        out_specs=[
            pl.BlockSpec((1, HALF, N_MAIN_BLK * BLK), lambda b, t: (b, t, 0)),
            pl.BlockSpec((1, TILE, 128), lambda b, t: (b, t, 0)),
        ],
        out_shape=[
            jax.ShapeDtypeStruct((bsz, lp // 2, N_MAIN_BLK * BLK), U32),
            jax.ShapeDtypeStruct((bsz, lp, 128), F32),
        ],
        scratch_shapes=[pltpu.VMEM((TILE + 8, 3 * BLK), F32),
                        pltpu.VMEM((DN_CONV - 1, TILE, 3 * BLK), F32)],
        compiler_params=cparams,
        name="proj",
    )(x, meta_tile, row(norm_w[0]), w_main, w_ba, conv_qkv_w[0].astype(F32))

    col = lambda j: pl.BlockSpec((1, HALF, BLK), lambda b, t, j=j: (b, t, j))
    og = pl.pallas_call(
        _delta_kernel,
        grid=(bsz, nt),
        in_specs=[
            col(0), col(1), col(2), col(3),
            pl.BlockSpec((1, TILE, 128), lambda b, t: (b, t, 0)),
            pl.BlockSpec((1, 128), const2),
            pl.BlockSpec((1, 128), const2),
            pl.BlockSpec((1, DN_DV), const2),
            pl.BlockSpec((2 * TILE, TILE), const2),
            pl.BlockSpec((2, TILE, TILE), const3),
            pl.BlockSpec((6, TILE, TILE), const3),
        ],
        out_specs=pl.BlockSpec((1, HALF, BLK), lambda b, t: (b, t, 0)),
        out_shape=jax.ShapeDtypeStruct((bsz, lp // 2, BLK), U32),
        scratch_shapes=[pltpu.VMEM((DN_HEADS, DN_DK, DN_DV), F32)],
        compiler_params=cparams,
        name="delta",
    )(proj, proj, proj, proj, ba, alog_row, dtb_row,
      row(dn_norm_w[0]), jnp.asarray(tri), jnp.asarray(msk), jnp.asarray(lvls, BF16))

    wspec = pl.BlockSpec((D_MODEL // 2, D_MODEL), const2)
    vspec = pl.BlockSpec((1, D_MODEL), const2)
    out = pl.pallas_call(
        _merge_kernel,
        grid=(bsz, nt),
        in_specs=[
            pl.BlockSpec((1, HALF, BLK), lambda b, t: (b, t, 0)),
            col(4), col(5), col(6), col(7), col(8),
            pl.BlockSpec((1, TILE, D_MODEL), xmap),
            wspec, wspec, wspec,
            pl.BlockSpec((32, D_MODEL), const2),
            vspec, vspec, vspec, vspec, vspec,
        ],
        out_specs=pl.BlockSpec((1, TILE, D_MODEL), xmap),
        out_shape=jax.ShapeDtypeStruct((bsz, seq, D_MODEL), F32),
        scratch_shapes=[
            pltpu.VMEM((DW_HALO + TILE, BLK), F32),
            pltpu.VMEM((TILE, BLK), F32),
            pltpu.VMEM((2, 7, SH_ROWS, 128), F32),
        ],
        compiler_params=cparams,
        name="merge",
    )(og, proj, proj, proj, proj, proj, x,
      _pack_rows(w_dn_out[0].astype(BF16)), _pack_rows(w_cf_out[0].astype(BF16)),
      _pack_rows(w_o[0].astype(BF16)),
      dww, row(dw_b[0]), row(ln_w[0]), row(ln_b[0]), row(b_cf_out[0]), row(final_norm_w))
    return out
```

```python
import numpy as np
import jax
import jax.numpy as jnp
from jax import lax
from jax.experimental import pallas as pl
from jax.experimental.pallas import tpu as pltpu

D_MODEL = 1024
N_META = 16
EPS = 1e-6
DN_HEADS = 8
DN_DK = 128
DN_DV = 128
DN_CONV = 4
CHUNK = 64
CF_KERNEL = 31

TILE = 256
HALF = TILE // 2
CHUNKS_PER_TILE = TILE // CHUNK
BLK = 1024
N_MAIN_BLK = 9
DW_HALO = 32
SH_ROWS = DW_HALO + TILE - 8
VMEM_LIMIT = 56 * 1024 * 1024

F32 = jnp.float32
BF16 = jnp.bfloat16
U32 = jnp.uint32


def _unpack(ref_slice):
    return pltpu.bitcast(ref_slice, BF16)


def _pack(x):
    return pltpu.bitcast(x.astype(BF16), U32)


def _pack_rows(a):
    lo = lax.bitcast_convert_type(a[0::2], jnp.uint16).astype(U32)
    hi = lax.bitcast_convert_type(a[1::2], jnp.uint16).astype(U32)
    return (hi << 16) | lo


def _mm_nt(a, b):
    return lax.dot_general(a.astype(BF16), b.astype(BF16), (((1,), (1,)), ((), ())),
                           preferred_element_type=F32)


def _mm_tn(a, b):
    return lax.dot_general(a.astype(BF16), b.astype(BF16), (((0,), (0,)), ((), ())),
                           preferred_element_type=F32)


def _silu(x):
    return x * jax.nn.sigmoid(x)


def _proj_kernel(x_ref, meta_ref, nw_ref, w_ref, wba_ref, cw_ref, proj_ref, ba_ref, pre_ref, tap_ref):
    t = pl.program_id(1)
    last = DN_CONV - 1

    @pl.when(t == 0)
    def _():
        pre_ref[TILE:TILE + 8, :] = jnp.zeros((8, 3 * BLK), F32)

    xin = jnp.where(t == 0, meta_ref[...], x_ref[0])
    ms = jnp.mean(xin * xin, axis=-1, keepdims=True)
    h = (xin * lax.rsqrt(ms + EPS) * nw_ref[...]).astype(BF16)

    pre_ref[0:8, :] = pre_ref[TILE:TILE + 8, :]
    for j in range(3):
        cs = slice(j * BLK, (j + 1) * BLK)
        pre_ref[8:8 + TILE, cs] = jnp.dot(h, _unpack(w_ref[:, cs]), preferred_element_type=F32)
        for tap in range(last):
            d = last - tap
            tap_ref[tap, :, cs] = pre_ref[8 - d:8 - d + TILE, cs]

    def conv_block(kind, hh):
        col0 = kind * BLK + hh * DN_DK
        acc = cw_ref[last:last + 1, col0:col0 + DN_DK] * pre_ref[8:8 + TILE, col0:col0 + DN_DK]
        for tap in range(last):
            acc = acc + cw_ref[tap:tap + 1, col0:col0 + DN_DK] * tap_ref[tap, :, col0:col0 + DN_DK]
        y = _silu(acc)
        if kind == 0:
            y = y * lax.rsqrt(jnp.sum(y * y, axis=-1, keepdims=True) + EPS) * (DN_DK ** -0.5)
        elif kind == 1:
            y = y * lax.rsqrt(jnp.sum(y * y, axis=-1, keepdims=True) + EPS)
        proj_ref[0, :, col0:col0 + DN_DK] = _pack(y)

    blocks = [(kind, hh) for kind in range(3) for hh in range(DN_HEADS)]
    sub = (N_MAIN_BLK - 3) * BLK // len(blocks)
    for i, (kind, hh) in enumerate(blocks):
        cs = slice(3 * BLK + i * sub, 3 * BLK + (i + 1) * sub)
        proj_ref[0, :, cs] = _pack(jnp.dot(h, _unpack(w_ref[:, cs]), preferred_element_type=F32))
        conv_block(kind, hh)
    ba_ref[0] = jnp.dot(h, _unpack(wba_ref[...]), preferred_element_type=F32)


def _delta_kernel(q_ref, k_ref, v_ref, za_ref, ba_ref, alog_ref, dtb_ref, dnw_ref,
                  tri_ref, msk_ref, lvl_ref, og_ref, s_ref):
    t = pl.program_id(1)

    @pl.when(t == 0)
    def _():
        s_ref[...] = jnp.zeros_like(s_ref)

    ba = ba_ref[0]
    beta_all = jax.nn.sigmoid(ba)
    g_all = -jnp.exp(alog_ref[...]) * jax.nn.softplus(ba + dtb_ref[...])
    csum = jnp.dot(tri_ref[...], g_all, preferred_element_type=F32,
                   precision=lax.Precision.HIGHEST)
    gc_all = csum[0:TILE]
    gl_all = csum[TILE:2 * TILE]
    gc_t = gc_all.T
    egc_all = jnp.exp(gc_all)
    eend_all = jnp.exp(gl_all - gc_all)
    egl_all = jnp.exp(gl_all)

    neg = msk_ref[0]
    eye_bf = msk_ref[1].astype(BF16)
    heads = range(DN_HEADS)

    nm, tv, attn, rhs, qg, kend, egl = [], [], [], [], [], [], []
    for h in heads:
        c0 = h * DN_DK
        qn = _unpack(q_ref[0, :, c0:c0 + DN_DK]).astype(F32)
        kn_bf = _unpack(k_ref[0, :, c0:c0 + DN_DK])
        kn = kn_bf.astype(F32)
        vs = _unpack(v_ref[0, :, c0:c0 + DN_DV]).astype(F32)

        c_beta = beta_all[:, h:h + 1]
        c_gc = gc_all[:, DN_HEADS + h:DN_HEADS + h + 1]
        r_gc = gc_t[DN_HEADS + h:DN_HEADS + h + 1, :]
        c_egc = egc_all[:, DN_HEADS + h:DN_HEADS + h + 1]
        c_eend = eend_all[:, DN_HEADS + h:DN_HEADS + h + 1]
        egl.append(egl_all[:, DN_HEADS + h:DN_HEADS + h + 1])

        kb = kn * c_beta
        decay = jnp.exp((c_gc - r_gc) + neg)
        nm_h = (_mm_nt(kb, kn_bf) * decay).astype(BF16)
        nm.append(nm_h)
        attn.append((_mm_nt(qn, kn_bf) * decay).astype(BF16))
        rhs.append(jnp.concatenate([vs * c_beta, kb * c_egc], axis=1).astype(BF16))
        qg.append((qn * c_egc).astype(BF16))
        kend.append((kn * c_eend).astype(BF16))
        tv.append(eye_bf - nm_h * lvl_ref[0])

    for lv in range(1, 6):
        x1 = [jnp.dot(nm[h] * lvl_ref[lv], tv[h], preferred_element_type=F32).astype(BF16)
              for h in heads]
        tv = [tv[h] - jnp.dot(tv[h], x1[h], preferred_element_type=F32).astype(BF16)
              for h in heads]

    u, w = [], []
    for h in heads:
        sol = jnp.dot(tv[h], rhs[h], preferred_element_type=F32)
        u.append(sol[:, 0:DN_DV])
        w.append(sol[:, DN_DV:DN_DV + DN_DK].astype(BF16))

    s = [s_ref[h] for h in heads]
    wv = [[] for _ in heads]
    o_s = [[] for _ in heads]
    for c in range(CHUNKS_PER_TILE):
        rs = slice(c * CHUNK, (c + 1) * CHUNK)
        for h in heads:
            s_bf = s[h].astype(BF16)
            ws = jnp.dot(jnp.concatenate([w[h][rs], qg[h][rs]], axis=0), s_bf,
                         preferred_element_type=F32)
            wv_c = u[h][rs] - ws[0:CHUNK]
            o_s[h].append(ws[CHUNK:2 * CHUNK])
            s[h] = s[h] * egl[h][c * CHUNK:c * CHUNK + 1, :] + _mm_tn(kend[h][rs], wv_c)
            wv[h].append(wv_c.astype(BF16))

    for h in heads:
        c0 = h * DN_DK
        s_ref[h] = s[h]
        o = jnp.concatenate(o_s[h], axis=0) + jnp.dot(attn[h], jnp.concatenate(wv[h], axis=0),
                                                      preferred_element_type=F32)
        on = o * lax.rsqrt(jnp.mean(o * o, axis=-1, keepdims=True) + EPS) * dnw_ref[...]
        za = _unpack(za_ref[0, :, c0:c0 + DN_DV]).astype(F32)
        og_ref[0, :, c0:c0 + DN_DV] = _pack(on * _silu(za))


def _merge_kernel(og_ref, glua_ref, glub_ref, zb_ref, ga_ref, gb_ref, x_ref,
                  wdn_ref, wcf_ref, wo_ref, dww_ref, dwb_ref, lnw_ref, lnb_ref, bcf_ref, fnw_ref,
                  out_ref, cbuf_ref, conv_ref, sh_ref):
    t = pl.program_id(1)

    @pl.when(t == 0)
    def _():
        cbuf_ref[0:DW_HALO, :] = jnp.zeros((DW_HALO, BLK), F32)

    cbuf_ref[DW_HALO:DW_HALO + TILE, :] = (_unpack(glua_ref[0]).astype(F32)
                                           * jax.nn.sigmoid(_unpack(glub_ref[0]).astype(F32)))

    rb = 64
    first = DW_HALO - (CF_KERNEL - 1)
    for cb in range(BLK // 128):
        cs = slice(cb * 128, (cb + 1) * 128)
        slot = cb % 2
        for s in range(1, 8):
            sh_ref[slot, s - 1] = cbuf_ref[s:s + SH_ROWS, cs]
        for r in range(TILE // rb):
            acc = jnp.broadcast_to(dwb_ref[:, cs], (rb, 128))
            for j in range(CF_KERNEL):
                s = (first + j) % 8
                r0 = (first + j) - s + r * rb
                if s == 0:
                    rows = cbuf_ref[r0:r0 + rb, cs]
                else:
                    rows = sh_ref[slot, s - 1, r0:r0 + rb, :]
                acc = acc + dww_ref[j:j + 1, cs] * rows
            conv_ref[r * rb:(r + 1) * rb, cs] = acc

    cbuf_ref[0:DW_HALO, :] = cbuf_ref[TILE:TILE + DW_HALO, :]

    c = conv_ref[...]
    mu = jnp.mean(c, axis=-1, keepdims=True)
    cen = c - mu
    var = jnp.mean(cen * cen, axis=-1, keepdims=True)
    y = cen * lax.rsqrt(var + EPS) * lnw_ref[...] + lnb_ref[...]
    cc = _silu(y) * _silu(_unpack(zb_ref[0]).astype(F32))
    y_b = jnp.dot(cc.astype(BF16), _unpack(wcf_ref[...]), preferred_element_type=F32) + bcf_ref[...]
    y_a = jnp.dot(_unpack(og_ref[0]), _unpack(wdn_ref[...]), preferred_element_type=F32)
    merged = (jax.nn.sigmoid(_unpack(ga_ref[0]).astype(F32)) * y_a
              + jax.nn.sigmoid(_unpack(gb_ref[0]).astype(F32)) * y_b)
    xo = x_ref[0] + jnp.dot(merged.astype(BF16), _unpack(wo_ref[...]), preferred_element_type=F32)
    ms = jnp.mean(xo * xo, axis=-1, keepdims=True)
    out_ref[0] = xo * lax.rsqrt(ms + EPS) * fnw_ref[...]


def _tile_constants():
    i = np.arange(TILE)[:, None]
    j = np.arange(TILE)[None, :]
    same = (i // CHUNK) == (j // CHUNK)
    incl = (same & (i >= j)).astype(np.float32)
    tri = np.concatenate([incl, same.astype(np.float32)], axis=0)
    x = i ^ j
    lvls = []
    b = 1
    while b < CHUNK:
        lvls.append(((x >= b) & (x < 2 * b) & (i > j)).astype(np.float32))
        b *= 2
    neg = np.where(incl > 0, 0.0, -1e30).astype(np.float32)
    return tri, np.stack([neg, np.eye(TILE, dtype=np.float32)]), np.stack(lvls)


def kernel(x, meta, norm_w, w_in, conv_qkv_w, a_log, dt_bias, dn_norm_w, w_dn_out,
           dw_w, dw_b, ln_w, ln_b, w_cf_out, b_cf_out, w_o, final_norm_w):
    bsz, seq, d = x.shape
    assert d == D_MODEL and seq % TILE == 0 and w_in.shape[0] == 1
    nt = seq // TILE + 1
    lp = nt * TILE

    w = w_in[0]
    qkvz = 4 * BLK
    w_main = _pack_rows(jnp.concatenate([w[:, :qkvz], w[:, qkvz + 2 * DN_HEADS:]], axis=1).astype(BF16))
    w_ba = _pack_rows(jnp.pad(w[:, qkvz:qkvz + 2 * DN_HEADS],
                              ((0, 0), (0, 128 - 2 * DN_HEADS))).astype(BF16))
    meta_tile = jnp.pad(meta.astype(F32), ((TILE - N_META, 0), (0, 0)))
    row = lambda v: v.reshape(1, -1).astype(F32)
    alog_row = jnp.pad(row(a_log[0]), ((0, 0), (DN_HEADS, 128 - 2 * DN_HEADS)))
    dtb_row = jnp.pad(row(dt_bias[0]), ((0, 0), (DN_HEADS, 128 - 2 * DN_HEADS)))
    dww = jnp.pad(dw_w[0].astype(F32), ((0, 32 - CF_KERNEL), (0, 0)))
    tri, msk, lvls = _tile_constants()

    cparams = pltpu.CompilerParams(dimension_semantics=("arbitrary", "arbitrary"),
                                   vmem_limit_bytes=VMEM_LIMIT)
    xmap = lambda b, t: (b, jnp.maximum(t - 1, 0), 0)
    const2 = lambda b, t: (0, 0)
    const3 = lambda b, t: (0, 0, 0)

    proj, ba = pl.pallas_call(
        _proj_kernel,
        grid=(bsz, nt),
        in_specs=[
            pl.BlockSpec((1, TILE, D_MODEL), xmap),
            pl.BlockSpec((TILE, D_MODEL), const2),
            pl.BlockSpec((1, D_MODEL), const2),
            pl.BlockSpec((D_MODEL // 2, N_MAIN_BLK * BLK), const2, pipeline_mode=pl.Buffered(1)),
            pl.BlockSpec((D_MODEL // 2, 128), const2),
            pl.BlockSpec((DN_CONV, 3 * BLK), const2),
        ],
        out_specs=[
            pl.BlockSpec((1, HALF, N_MAIN_BLK * BLK), lambda b, t: (b, t, 0)),
            pl.BlockSpec((1, TILE, 128), lambda b, t: (b, t, 0)),
        ],
        out_shape=[
            jax.ShapeDtypeStruct((bsz, lp // 2, N_MAIN_BLK * BLK), U32),
            jax.ShapeDtypeStruct((bsz, lp, 128), F32),
        ],
        scratch_shapes=[pltpu.VMEM((TILE + 8, 3 * BLK), F32),
                        pltpu.VMEM((DN_CONV - 1, TILE, 3 * BLK), F32)],
        compiler_params=cparams,
        name="proj",
    )(x, meta_tile, row(norm_w[0]), w_main, w_ba, conv_qkv_w[0].astype(F32))

    col = lambda j: pl.BlockSpec((1, HALF, BLK), lambda b, t, j=j: (b, t, j))
    og = pl.pallas_call(
        _delta_kernel,
        grid=(bsz, nt),
        in_specs=[
            col(0), col(1), col(2), col(3),
            pl.BlockSpec((1, TILE, 128), lambda b, t: (b, t, 0)),
            pl.BlockSpec((1, 128), const2),
            pl.BlockSpec((1, 128), const2),
            pl.BlockSpec((1, DN_DV), const2),
            pl.BlockSpec((2 * TILE, TILE), const2),
            pl.BlockSpec((2, TILE, TILE), const3),
            pl.BlockSpec((6, TILE, TILE), const3),
        ],
        out_specs=pl.BlockSpec((1, HALF, BLK), lambda b, t: (b, t, 0)),
        out_shape=jax.ShapeDtypeStruct((bsz, lp // 2, BLK), U32),
        scratch_shapes=[pltpu.VMEM((DN_HEADS, DN_DK, DN_DV), F32)],
        compiler_params=cparams,
        name="delta",
    )(proj, proj, proj, proj, ba, alog_row, dtb_row,
      row(dn_norm_w[0]), jnp.asarray(tri), jnp.asarray(msk), jnp.asarray(lvls, BF16))

    wspec = pl.BlockSpec((D_MODEL // 2, D_MODEL), const2)
    vspec = pl.BlockSpec((1, D_MODEL), const2)
    out = pl.pallas_call(
        _merge_kernel,
        grid=(bsz, nt),
        in_specs=[
            pl.BlockSpec((1, HALF, BLK), lambda b, t: (b, t, 0)),
            col(4), col(5), col(6), col(7), col(8),
            pl.BlockSpec((1, TILE, D_MODEL), xmap),
            wspec, wspec, wspec,
            pl.BlockSpec((32, D_MODEL), const2),
            vspec, vspec, vspec, vspec, vspec,
        ],
        out_specs=pl.BlockSpec((1, TILE, D_MODEL), xmap),
        out_shape=jax.ShapeDtypeStruct((bsz, seq, D_MODEL), F32),
        scratch_shapes=[
            pltpu.VMEM((DW_HALO + TILE, BLK), F32),
            pltpu.VMEM((TILE, BLK), F32),
            pltpu.VMEM((2, 7, SH_ROWS, 128), F32),
        ],
        compiler_params=cparams,
        name="merge",
    )(og, proj, proj, proj, proj, proj, x,
      _pack_rows(w_dn_out[0].astype(BF16)), _pack_rows(w_cf_out[0].astype(BF16)),
      _pack_rows(w_o[0].astype(BF16)),
      dww, row(dw_b[0]), row(ln_w[0]), row(ln_b[0]), row(b_cf_out[0]), row(final_norm_w))
    return out
```

```python
import numpy as np
import jax
import jax.numpy as jnp
from jax import lax
from jax.experimental import pallas as pl
from jax.experimental.pallas import tpu as pltpu

D_MODEL = 1024
N_META = 16
EPS = 1e-6
DN_HEADS = 8
DN_DK = 128
DN_DV = 128
DN_CONV = 4
CHUNK = 64
CF_KERNEL = 31

TILE = 256
HALF = TILE // 2
CHUNKS_PER_TILE = TILE // CHUNK
BLK = 1024
N_MAIN_BLK = 9
CONV_RB = 64
DW_HALO = 32
SH_ROWS = DW_HALO + TILE - 8
VMEM_LIMIT = 56 * 1024 * 1024

F32 = jnp.float32
BF16 = jnp.bfloat16
U32 = jnp.uint32


def _unpack(ref_slice):
    return pltpu.bitcast(ref_slice, BF16)


def _pack(x):
    return pltpu.bitcast(x.astype(BF16), U32)


def _pack_rows(a):
    lo = lax.bitcast_convert_type(a[0::2], jnp.uint16).astype(U32)
    hi = lax.bitcast_convert_type(a[1::2], jnp.uint16).astype(U32)
    return (hi << 16) | lo


def _pack_kernel(w_ref, o_ref):
    o_ref[...] = _pack(w_ref[...])


def _pack_weight(w, name):
    k, n = w.shape
    return pl.pallas_call(
        _pack_kernel,
        grid=(n // BLK,),
        in_specs=[pl.BlockSpec((k, BLK), lambda j: (0, j))],
        out_specs=pl.BlockSpec((k // 2, BLK), lambda j: (0, j)),
        out_shape=jax.ShapeDtypeStruct((k // 2, n), U32),
        name=name,
    )(w.astype(F32))


def _mm_nt(a, b):
    return lax.dot_general(a.astype(BF16), b.astype(BF16), (((1,), (1,)), ((), ())),
                           preferred_element_type=F32)


def _mm_tn(a, b):
    return lax.dot_general(a.astype(BF16), b.astype(BF16), (((0,), (0,)), ((), ())),
                           preferred_element_type=F32)


def _silu(x):
    return x * jax.nn.sigmoid(x)


def _proj_kernel(x_ref, meta_ref, nw_ref, w_ref, wba_ref, cw_ref, proj_ref, ba_ref, pre_ref, tap_ref):
    t = pl.program_id(1)
    last = DN_CONV - 1

    @pl.when(t == 0)
    def _():
        pre_ref[TILE:TILE + 8, :] = jnp.zeros((8, 3 * BLK), F32)

    xin = jnp.where(t == 0, meta_ref[...], x_ref[0])
    ms = jnp.mean(xin * xin, axis=-1, keepdims=True)
    h = (xin * lax.rsqrt(ms + EPS) * nw_ref[...]).astype(BF16)

    pre_ref[0:8, :] = pre_ref[TILE:TILE + 8, :]
    for j in range(3):
        cs = slice(j * BLK, (j + 1) * BLK)
        pre_ref[8:8 + TILE, cs] = jnp.dot(h, _unpack(w_ref[:, cs]), preferred_element_type=F32)
        for tap in range(last):
            d = last - tap
            tap_ref[tap, :, cs] = pre_ref[8 - d:8 - d + TILE, cs]

    def conv_block(kind, hh):
        col0 = kind * BLK + hh * DN_DK
        cols = slice(col0, col0 + DN_DK)
        for r0 in range(0, TILE, CONV_RB):
            acc = cw_ref[last:last + 1, cols] * pre_ref[8 + r0:8 + r0 + CONV_RB, cols]
            for tap in range(last):
                acc = acc + cw_ref[tap:tap + 1, cols] * tap_ref[tap, r0:r0 + CONV_RB, cols]
            y = _silu(acc)
            if kind == 0:
                y = y * lax.rsqrt(jnp.sum(y * y, axis=-1, keepdims=True) + EPS) * (DN_DK ** -0.5)
            elif kind == 1:
                y = y * lax.rsqrt(jnp.sum(y * y, axis=-1, keepdims=True) + EPS)
            proj_ref[0, r0 // 2:(r0 + CONV_RB) // 2, cols] = _pack(y)

    blocks = [(kind, hh) for kind in range(3) for hh in range(DN_HEADS)]
    sub = (N_MAIN_BLK - 3) * BLK // len(blocks)
    for i, (kind, hh) in enumerate(blocks):
        cs = slice(3 * BLK + i * sub, 3 * BLK + (i + 1) * sub)
        proj_ref[0, :, cs] = _pack(jnp.dot(h, _unpack(w_ref[:, cs]), preferred_element_type=F32))
        conv_block(kind, hh)
    ba_ref[0] = jnp.dot(h, _unpack(wba_ref[...]), preferred_element_type=F32)


def _delta_kernel(q_ref, k_ref, v_ref, za_ref, ba_ref, alog_ref, dtb_ref, dnw_ref,
                  tri_ref, msk_ref, lvl_ref, og_ref, s_ref):
    t = pl.program_id(1)

    @pl.when(t == 0)
    def _():
        s_ref[...] = jnp.zeros_like(s_ref)

    ba = ba_ref[0]
    beta_all = jax.nn.sigmoid(ba)
    g_all = -jnp.exp(alog_ref[...]) * jax.nn.softplus(ba + dtb_ref[...])
    g_hi = g_all.astype(BF16)
    g_r1 = g_all - g_hi.astype(F32)
    g_mid = g_r1.astype(BF16)
    g_lo = (g_r1 - g_mid.astype(F32)).astype(BF16)
    tri = tri_ref[...]
    gc_all = (jnp.dot(tri, g_hi, preferred_element_type=F32)
              + jnp.dot(tri, g_mid, preferred_element_type=F32)
              + jnp.dot(tri, g_lo, preferred_element_type=F32))
    gl_all = jnp.concatenate(
        [jnp.broadcast_to(gc_all[(c + 1) * CHUNK - 1:(c + 1) * CHUNK, :], (CHUNK, 128))
         for c in range(CHUNKS_PER_TILE)], axis=0)
    gc_t = gc_all.T
    egc_all = jnp.exp(gc_all)
    eend_all = jnp.exp(gl_all - gc_all)
    egl_all = jnp.exp(gl_all)

    neg = msk_ref[0]
    eye_bf = msk_ref[1].astype(BF16)
    heads = range(DN_HEADS)

    nm, tv, attn, rhs, qg, kend, egl = [], [], [], [], [], [], []
    for h in heads:
        c0 = h * DN_DK
        qn = _unpack(q_ref[0, :, c0:c0 + DN_DK]).astype(F32)
        kn_bf = _unpack(k_ref[0, :, c0:c0 + DN_DK])
        kn = kn_bf.astype(F32)
        vs = _unpack(v_ref[0, :, c0:c0 + DN_DV]).astype(F32)

        c_beta = beta_all[:, h:h + 1]
        c_gc = gc_all[:, DN_HEADS + h:DN_HEADS + h + 1]
        r_gc = gc_t[DN_HEADS + h:DN_HEADS + h + 1, :]
        c_egc = egc_all[:, DN_HEADS + h:DN_HEADS + h + 1]
        c_eend = eend_all[:, DN_HEADS + h:DN_HEADS + h + 1]
        egl.append(egl_all[:, DN_HEADS + h:DN_HEADS + h + 1])

        kb = kn * c_beta
        decay = jnp.exp((c_gc - r_gc) + neg)
        nm_h = (_mm_nt(kb, kn_bf) * decay).astype(BF16)
        nm.append(nm_h)
        attn.append((_mm_nt(qn, kn_bf) * decay).astype(BF16))
        rhs.append(jnp.concatenate([vs * c_beta, kb * c_egc], axis=1).astype(BF16))
        qg.append((qn * c_egc).astype(BF16))
        kend.append((kn * c_eend).astype(BF16))
        tv.append(eye_bf - nm_h * lvl_ref[0])

    for lv in range(1, 6):
        x1 = [jnp.dot(nm[h] * lvl_ref[lv], tv[h], preferred_element_type=F32).astype(BF16)
              for h in heads]
        tv = [tv[h] - jnp.dot(tv[h], x1[h], preferred_element_type=F32).astype(BF16)
              for h in heads]

    u, w = [], []
    for h in heads:
        sol = jnp.dot(tv[h], rhs[h], preferred_element_type=F32)
        u.append(sol[:, 0:DN_DV])
        w.append(sol[:, DN_DV:DN_DV + DN_DK].astype(BF16))

    s = [s_ref[h] for h in heads]
    wv = [[] for _ in heads]
    o_s = [[] for _ in heads]
    for c in range(CHUNKS_PER_TILE):
        rs = slice(c * CHUNK, (c + 1) * CHUNK)
        for h in heads:
            s_bf = s[h].astype(BF16)
            ws = jnp.dot(jnp.concatenate([w[h][rs], qg[h][rs]], axis=0), s_bf,
                         preferred_element_type=F32)
            wv_c = u[h][rs] - ws[0:CHUNK]
            o_s[h].append(ws[CHUNK:2 * CHUNK])
            s[h] = s[h] * egl[h][c * CHUNK:c * CHUNK + 1, :] + _mm_tn(kend[h][rs], wv_c)
            wv[h].append(wv_c.astype(BF16))

    for h in heads:
        c0 = h * DN_DK
        s_ref[h] = s[h]
        o = jnp.concatenate(o_s[h], axis=0) + jnp.dot(attn[h], jnp.concatenate(wv[h], axis=0),
                                                      preferred_element_type=F32)
        on = o * lax.rsqrt(jnp.mean(o * o, axis=-1, keepdims=True) + EPS) * dnw_ref[...]
        za = _unpack(za_ref[0, :, c0:c0 + DN_DV]).astype(F32)
        og_ref[0, :, c0:c0 + DN_DV] = _pack(on * _silu(za))


def _merge_kernel(og_ref, glua_ref, glub_ref, zb_ref, ga_ref, gb_ref, x_ref,
                  wdn_ref, wcf_ref, wo_ref, dww_ref, dwb_ref, lnw_ref, lnb_ref, bcf_ref, fnw_ref,
                  out_ref, cbuf_ref, conv_ref, sh_ref):
    t = pl.program_id(1)

    @pl.when(t == 0)
    def _():
        cbuf_ref[0:DW_HALO, :] = jnp.zeros((DW_HALO, BLK), F32)

    cbuf_ref[DW_HALO:DW_HALO + TILE, :] = (_unpack(glua_ref[0]).astype(F32)
                                           * jax.nn.sigmoid(_unpack(glub_ref[0]).astype(F32)))

    rb = 64
    first = DW_HALO - (CF_KERNEL - 1)
    for cb in range(BLK // 128):
        cs = slice(cb * 128, (cb + 1) * 128)
        slot = cb % 2
        for s in range(1, 8):
            sh_ref[slot, s - 1] = cbuf_ref[s:s + SH_ROWS, cs]
        for r in range(TILE // rb):
            acc = jnp.broadcast_to(dwb_ref[:, cs], (rb, 128))
            for j in range(CF_KERNEL):
                s = (first + j) % 8
                r0 = (first + j) - s + r * rb
                if s == 0:
                    rows = cbuf_ref[r0:r0 + rb, cs]
                else:
                    rows = sh_ref[slot, s - 1, r0:r0 + rb, :]
                acc = acc + dww_ref[j:j + 1, cs] * rows
            conv_ref[r * rb:(r + 1) * rb, cs] = acc

    cbuf_ref[0:DW_HALO, :] = cbuf_ref[TILE:TILE + DW_HALO, :]

    c = conv_ref[...]
    mu = jnp.mean(c, axis=-1, keepdims=True)
    cen = c - mu
    var = jnp.mean(cen * cen, axis=-1, keepdims=True)
    y = cen * lax.rsqrt(var + EPS) * lnw_ref[...] + lnb_ref[...]
    cc = _silu(y) * _silu(_unpack(zb_ref[0]).astype(F32))
    y_b = jnp.dot(cc.astype(BF16), _unpack(wcf_ref[...]), preferred_element_type=F32) + bcf_ref[...]
    y_a = jnp.dot(_unpack(og_ref[0]), _unpack(wdn_ref[...]), preferred_element_type=F32)
    merged = (jax.nn.sigmoid(_unpack(ga_ref[0]).astype(F32)) * y_a
              + jax.nn.sigmoid(_unpack(gb_ref[0]).astype(F32)) * y_b)
    xo = x_ref[0] + jnp.dot(merged.astype(BF16), _unpack(wo_ref[...]), preferred_element_type=F32)
    ms = jnp.mean(xo * xo, axis=-1, keepdims=True)
    out_ref[0] = xo * lax.rsqrt(ms + EPS) * fnw_ref[...]


def _tile_constants():
    i = np.arange(TILE)[:, None]
    j = np.arange(TILE)[None, :]
    same = (i // CHUNK) == (j // CHUNK)
    incl = (same & (i >= j)).astype(np.float32)
    tri = incl
    x = i ^ j
    lvls = []
    b = 1
    while b < CHUNK:
        lvls.append(((x >= b) & (x < 2 * b) & (i > j)).astype(np.float32))
        b *= 2
    neg = np.where(incl > 0, 0.0, -1e30).astype(np.float32)
    return tri, np.stack([neg, np.eye(TILE, dtype=np.float32)]), np.stack(lvls)


def kernel(x, meta, norm_w, w_in, conv_qkv_w, a_log, dt_bias, dn_norm_w, w_dn_out,
           dw_w, dw_b, ln_w, ln_b, w_cf_out, b_cf_out, w_o, final_norm_w):
    bsz, seq, d = x.shape
    assert d == D_MODEL and seq % TILE == 0 and w_in.shape[0] == 1
    nt = seq // TILE + 1
    lp = nt * TILE

    w = w_in[0]
    qkvz = 4 * BLK
    w_main = _pack_weight(jnp.concatenate([w[:, :qkvz], w[:, qkvz + 2 * DN_HEADS:]], axis=1), "pack_w_in")
    w_ba = _pack_rows(jnp.pad(w[:, qkvz:qkvz + 2 * DN_HEADS],
                              ((0, 0), (0, 128 - 2 * DN_HEADS))).astype(BF16))
    meta_tile = jnp.pad(meta.astype(F32), ((TILE - N_META, 0), (0, 0)))
    row = lambda v: v.reshape(1, -1).astype(F32)
    alog_row = jnp.pad(row(a_log[0]), ((0, 0), (DN_HEADS, 128 - 2 * DN_HEADS)))
    dtb_row = jnp.pad(row(dt_bias[0]), ((0, 0), (DN_HEADS, 128 - 2 * DN_HEADS)))
    dww = jnp.pad(dw_w[0].astype(F32), ((0, 32 - CF_KERNEL), (0, 0)))
    tri, msk, lvls = _tile_constants()

    cparams = pltpu.CompilerParams(dimension_semantics=("arbitrary", "arbitrary"),
                                   vmem_limit_bytes=VMEM_LIMIT)
    xmap = lambda b, t: (b, jnp.maximum(t - 1, 0), 0)
    const2 = lambda b, t: (0, 0)
    const3 = lambda b, t: (0, 0, 0)

    proj, ba = pl.pallas_call(
        _proj_kernel,
        grid=(bsz, nt),
        in_specs=[
            pl.BlockSpec((1, TILE, D_MODEL), xmap),
            pl.BlockSpec((TILE, D_MODEL), const2),
            pl.BlockSpec((1, D_MODEL), const2),
            pl.BlockSpec((D_MODEL // 2, N_MAIN_BLK * BLK), const2, pipeline_mode=pl.Buffered(1)),
            pl.BlockSpec((D_MODEL // 2, 128), const2),
            pl.BlockSpec((DN_CONV, 3 * BLK), const2),
        ],
        out_specs=[
            pl.BlockSpec((1, HALF, N_MAIN_BLK * BLK), lambda b, t: (b, t, 0)),
            pl.BlockSpec((1, TILE, 128), lambda b, t: (b, t, 0)),
        ],
        out_shape=[
            jax.ShapeDtypeStruct((bsz, lp // 2, N_MAIN_BLK * BLK), U32),
            jax.ShapeDtypeStruct((bsz, lp, 128), F32),
        ],
        scratch_shapes=[pltpu.VMEM((TILE + 8, 3 * BLK), F32),
                        pltpu.VMEM((DN_CONV - 1, TILE, 3 * BLK), F32)],
        compiler_params=cparams,
        name="proj",
    )(x, meta_tile, row(norm_w[0]), w_main, w_ba, conv_qkv_w[0].astype(F32))

    col = lambda j: pl.BlockSpec((1, HALF, BLK), lambda b, t, j=j: (b, t, j))
    og = pl.pallas_call(
        _delta_kernel,
        grid=(bsz, nt),
        in_specs=[
            col(0), col(1), col(2), col(3),
            pl.BlockSpec((1, TILE, 128), lambda b, t: (b, t, 0)),
            pl.BlockSpec((1, 128), const2),
            pl.BlockSpec((1, 128), const2),
            pl.BlockSpec((1, DN_DV), const2),
            pl.BlockSpec((TILE, TILE), const2),
            pl.BlockSpec((2, TILE, TILE), const3),
            pl.BlockSpec((6, TILE, TILE), const3),
        ],
        out_specs=pl.BlockSpec((1, HALF, BLK), lambda b, t: (b, t, 0)),
        out_shape=jax.ShapeDtypeStruct((bsz, lp // 2, BLK), U32),
        scratch_shapes=[pltpu.VMEM((DN_HEADS, DN_DK, DN_DV), F32)],
        compiler_params=cparams,
        name="delta",
    )(proj, proj, proj, proj, ba, alog_row, dtb_row,
      row(dn_norm_w[0]), jnp.asarray(tri, BF16), jnp.asarray(msk), jnp.asarray(lvls, BF16))

    wspec = pl.BlockSpec((D_MODEL // 2, D_MODEL), const2)
    vspec = pl.BlockSpec((1, D_MODEL), const2)
    out = pl.pallas_call(
        _merge_kernel,
        grid=(bsz, nt),
        in_specs=[
            pl.BlockSpec((1, HALF, BLK), lambda b, t: (b, t, 0)),
            col(4), col(5), col(6), col(7), col(8),
            pl.BlockSpec((1, TILE, D_MODEL), xmap),
            wspec, wspec, wspec,
            pl.BlockSpec((32, D_MODEL), const2),
            vspec, vspec, vspec, vspec, vspec,
        ],
        out_specs=pl.BlockSpec((1, TILE, D_MODEL), xmap),
        out_shape=jax.ShapeDtypeStruct((bsz, seq, D_MODEL), F32),
        scratch_shapes=[
            pltpu.VMEM((DW_HALO + TILE, BLK), F32),
            pltpu.VMEM((TILE, BLK), F32),
            pltpu.VMEM((2, 7, SH_ROWS, 128), F32),
        ],
        compiler_params=cparams,
        name="merge",
    )(og, proj, proj, proj, proj, proj, x,
      _pack_weight(w_dn_out[0], "pack_w_dn"), _pack_weight(w_cf_out[0], "pack_w_cf"),
      _pack_weight(w_o[0], "pack_w_o"),
      dww, row(dw_b[0]), row(ln_w[0]), row(ln_b[0]), row(b_cf_out[0]), row(final_norm_w))
    return out
```

```python
import numpy as np
import jax
import jax.numpy as jnp
from jax import lax
from jax.experimental import pallas as pl
from jax.experimental.pallas import tpu as pltpu

D_MODEL = 1024
N_META = 16
EPS = 1e-6
DN_HEADS = 8
DN_DK = 128
DN_DV = 128
DN_CONV = 4
CHUNK = 64
CF_KERNEL = 31

TILE = 256
HALF = TILE // 2
CHUNKS_PER_TILE = TILE // CHUNK
BLK = 1024
N_MAIN_BLK = 9
CONV_RB = 64
DW_HALO = 32
SH_ROWS = DW_HALO + TILE - 8
VMEM_LIMIT = 56 * 1024 * 1024

F32 = jnp.float32
BF16 = jnp.bfloat16
U32 = jnp.uint32


def _unpack(ref_slice):
    return pltpu.bitcast(ref_slice, BF16)


def _pack(x):
    return pltpu.bitcast(x.astype(BF16), U32)


def _pack_kernel(w_ref, o_ref):
    o_ref[...] = _pack(w_ref[...])


def _pack_weight(w, name):
    k, n = w.shape
    return pl.pallas_call(
        _pack_kernel,
        grid=(n // BLK,),
        in_specs=[pl.BlockSpec((k, BLK), lambda j: (0, j))],
        out_specs=pl.BlockSpec((k // 2, BLK), lambda j: (0, j)),
        out_shape=jax.ShapeDtypeStruct((k // 2, n), U32),
        name=name,
    )(w.astype(F32))


def _pack_w_in_kernel(w_ref, main_ref, ba_ref):
    x = w_ref[...]
    qkvz = 4 * BLK
    nba = 2 * DN_HEADS
    main_ref[:, 0:qkvz] = _pack(x[:, 0:qkvz])
    main_ref[:, qkvz:] = _pack(x[:, qkvz + nba:])
    ba = jnp.concatenate([x[:, qkvz:qkvz + nba], jnp.zeros((x.shape[0], 128 - nba), F32)], axis=1)
    ba_ref[...] = _pack(ba)


def _pack_w_in(w):
    k, n = w.shape
    rb = 128
    return pl.pallas_call(
        _pack_w_in_kernel,
        grid=(k // rb,),
        in_specs=[pl.BlockSpec((rb, n), lambda i: (i, 0))],
        out_specs=[pl.BlockSpec((rb // 2, N_MAIN_BLK * BLK), lambda i: (i, 0)),
                   pl.BlockSpec((rb // 2, 128), lambda i: (i, 0))],
        out_shape=[jax.ShapeDtypeStruct((k // 2, N_MAIN_BLK * BLK), U32),
                   jax.ShapeDtypeStruct((k // 2, 128), U32)],
        name="pack_w_in",
    )(w.astype(F32))


def _mm_nt(a, b):
    return lax.dot_general(a.astype(BF16), b.astype(BF16), (((1,), (1,)), ((), ())),
                           preferred_element_type=F32)


def _mm_tn(a, b):
    return lax.dot_general(a.astype(BF16), b.astype(BF16), (((0,), (0,)), ((), ())),
                           preferred_element_type=F32)


def _silu(x):
    return x * jax.nn.sigmoid(x)


def _proj_kernel(x_ref, meta_ref, nw_ref, w_ref, wba_ref, cw_ref, proj_ref, ba_ref, pre_ref, tap_ref):
    t = pl.program_id(1)
    last = DN_CONV - 1

    @pl.when(t == 0)
    def _():
        pre_ref[TILE:TILE + 8, :] = jnp.zeros((8, 3 * BLK), F32)

    xin = jnp.where(t == 0, meta_ref[...], x_ref[0])
    ms = jnp.mean(xin * xin, axis=-1, keepdims=True)
    h = (xin * lax.rsqrt(ms + EPS) * nw_ref[...]).astype(BF16)

    pre_ref[0:8, :] = pre_ref[TILE:TILE + 8, :]
    for j in range(3):
        cs = slice(j * BLK, (j + 1) * BLK)
        pre_ref[8:8 + TILE, cs] = jnp.dot(h, _unpack(w_ref[:, cs]), preferred_element_type=F32)
        for tap in range(last):
            d = last - tap
            for hh in range(DN_HEADS):
                c0 = j * BLK + hh * DN_DK
                tap_ref[tap, j * DN_HEADS + hh] = pre_ref[8 - d:8 - d + TILE, c0:c0 + DN_DK]

    def conv_block(kind, hh):
        col0 = kind * BLK + hh * DN_DK
        cols = slice(col0, col0 + DN_DK)
        for r0 in range(0, TILE, CONV_RB):
            acc = cw_ref[last:last + 1, cols] * pre_ref[8 + r0:8 + r0 + CONV_RB, cols]
            for tap in range(last):
                acc = acc + cw_ref[tap:tap + 1, cols] * tap_ref[tap, kind * DN_HEADS + hh, r0:r0 + CONV_RB, :]
            y = _silu(acc)
            if kind == 0:
                y = y * lax.rsqrt(jnp.sum(y * y, axis=-1, keepdims=True) + EPS) * (DN_DK ** -0.5)
            elif kind == 1:
                y = y * lax.rsqrt(jnp.sum(y * y, axis=-1, keepdims=True) + EPS)
            proj_ref[0, r0 // 2:(r0 + CONV_RB) // 2, cols] = _pack(y)

    blocks = [(kind, hh) for kind in range(3) for hh in range(DN_HEADS)]
    sub = (N_MAIN_BLK - 3) * BLK // len(blocks)
    for i, (kind, hh) in enumerate(blocks):
        cs = slice(3 * BLK + i * sub, 3 * BLK + (i + 1) * sub)
        proj_ref[0, :, cs] = _pack(jnp.dot(h, _unpack(w_ref[:, cs]), preferred_element_type=F32))
        conv_block(kind, hh)
    ba_ref[0] = jnp.dot(h, _unpack(wba_ref[...]), preferred_element_type=F32)


def _delta_kernel(q_ref, k_ref, v_ref, za_ref, ba_ref, alog_ref, dtb_ref, dnw_ref,
                  tri_ref, msk_ref, lvl_ref, og_ref, s_ref):
    t = pl.program_id(1)

    @pl.when(t == 0)
    def _():
        s_ref[...] = jnp.zeros_like(s_ref)

    ba = ba_ref[0]
    beta_all = jax.nn.sigmoid(ba)
    g_all = -jnp.exp(alog_ref[...]) * jax.nn.softplus(ba + dtb_ref[...])
    g_hi = g_all.astype(BF16)
    g_r1 = g_all - g_hi.astype(F32)
    g_mid = g_r1.astype(BF16)
    g_lo = (g_r1 - g_mid.astype(F32)).astype(BF16)
    tri = tri_ref[...]
    gc_all = (jnp.dot(tri, g_hi, preferred_element_type=F32)
              + jnp.dot(tri, g_mid, preferred_element_type=F32)
              + jnp.dot(tri, g_lo, preferred_element_type=F32))
    gl_all = jnp.concatenate(
        [jnp.broadcast_to(gc_all[(c + 1) * CHUNK - 1:(c + 1) * CHUNK, :], (CHUNK, 128))
         for c in range(CHUNKS_PER_TILE)], axis=0)
    gc_t = gc_all.T
    egc_all = jnp.exp(gc_all)
    eend_all = jnp.exp(gl_all - gc_all)
    egl_all = jnp.exp(gl_all)

    neg = msk_ref[0]
    eye_bf = msk_ref[1].astype(BF16)
    heads = range(DN_HEADS)

    nm, tv, attn, rhs, qg, kend, egl = [], [], [], [], [], [], []
    for h in heads:
        c0 = h * DN_DK
        qn = _unpack(q_ref[0, :, c0:c0 + DN_DK]).astype(F32)
        kn_bf = _unpack(k_ref[0, :, c0:c0 + DN_DK])
        kn = kn_bf.astype(F32)
        vs = _unpack(v_ref[0, :, c0:c0 + DN_DV]).astype(F32)

        c_beta = beta_all[:, h:h + 1]
        c_gc = gc_all[:, DN_HEADS + h:DN_HEADS + h + 1]
        r_gc = gc_t[DN_HEADS + h:DN_HEADS + h + 1, :]
        c_egc = egc_all[:, DN_HEADS + h:DN_HEADS + h + 1]
        c_eend = eend_all[:, DN_HEADS + h:DN_HEADS + h + 1]
        egl.append(egl_all[:, DN_HEADS + h:DN_HEADS + h + 1])

        kb = kn * c_beta
        decay = jnp.exp((c_gc - r_gc) + neg)
        nm_h = (_mm_nt(kb, kn_bf) * decay).astype(BF16)
        nm.append(nm_h)
        attn.append((_mm_nt(qn, kn_bf) * decay).astype(BF16))
        rhs.append(jnp.concatenate([vs * c_beta, kb * c_egc], axis=1).astype(BF16))
        qg.append((qn * c_egc).astype(BF16))
        kend.append((kn * c_eend).astype(BF16))
        tv.append(eye_bf - nm_h * lvl_ref[0])

    for lv in range(1, 6):
        x1 = [jnp.dot(nm[h] * lvl_ref[lv], tv[h], preferred_element_type=F32).astype(BF16)
              for h in heads]
        tv = [tv[h] - jnp.dot(tv[h], x1[h], preferred_element_type=F32).astype(BF16)
              for h in heads]

    u, w = [], []
    for h in heads:
        sol = jnp.dot(tv[h], rhs[h], preferred_element_type=F32)
        u.append(sol[:, 0:DN_DV])
        w.append(sol[:, DN_DV:DN_DV + DN_DK].astype(BF16))

    s = [s_ref[h] for h in heads]
    wv = [[] for _ in heads]
    o_s = [[] for _ in heads]
    for c in range(CHUNKS_PER_TILE):
        rs = slice(c * CHUNK, (c + 1) * CHUNK)
        for h in heads:
            s_bf = s[h].astype(BF16)
            ws = jnp.dot(jnp.concatenate([w[h][rs], qg[h][rs]], axis=0), s_bf,
                         preferred_element_type=F32)
            wv_c = u[h][rs] - ws[0:CHUNK]
            o_s[h].append(ws[CHUNK:2 * CHUNK])
            s[h] = s[h] * egl[h][c * CHUNK:c * CHUNK + 1, :] + _mm_tn(kend[h][rs], wv_c)
            wv[h].append(wv_c.astype(BF16))

    for h in heads:
        c0 = h * DN_DK
        s_ref[h] = s[h]
        o = jnp.concatenate(o_s[h], axis=0) + jnp.dot(attn[h], jnp.concatenate(wv[h], axis=0),
                                                      preferred_element_type=F32)
        on = o * lax.rsqrt(jnp.mean(o * o, axis=-1, keepdims=True) + EPS) * dnw_ref[...]
        za = _unpack(za_ref[0, :, c0:c0 + DN_DV]).astype(F32)
        og_ref[0, :, c0:c0 + DN_DV] = _pack(on * _silu(za))


def _merge_kernel(og_ref, glua_ref, glub_ref, zb_ref, ga_ref, gb_ref, x_ref,
                  wdn_ref, wcf_ref, wo_ref, dww_ref, dwb_ref, lnw_ref, lnb_ref, bcf_ref, fnw_ref,
                  out_ref, cbuf_ref, conv_ref, sh_ref):
    t = pl.program_id(1)

    @pl.when(t == 0)
    def _():
        cbuf_ref[0:DW_HALO, :] = jnp.zeros((DW_HALO, BLK), F32)

    cbuf_ref[DW_HALO:DW_HALO + TILE, :] = (_unpack(glua_ref[0]).astype(F32)
                                           * jax.nn.sigmoid(_unpack(glub_ref[0]).astype(F32)))

    rb = 64
    first = DW_HALO - (CF_KERNEL - 1)
    for cb in range(BLK // 128):
        cs = slice(cb * 128, (cb + 1) * 128)
        slot = cb % 2
        for s in range(1, 8):
            sh_ref[slot, s - 1] = cbuf_ref[s:s + SH_ROWS, cs]
        for r in range(TILE // rb):
            acc = jnp.broadcast_to(dwb_ref[:, cs], (rb, 128))
            for j in range(CF_KERNEL):
                s = (first + j) % 8
                r0 = (first + j) - s + r * rb
                if s == 0:
                    rows = cbuf_ref[r0:r0 + rb, cs]
                else:
                    rows = sh_ref[slot, s - 1, r0:r0 + rb, :]
                acc = acc + dww_ref[j:j + 1, cs] * rows
            conv_ref[r * rb:(r + 1) * rb, cs] = acc

    cbuf_ref[0:DW_HALO, :] = cbuf_ref[TILE:TILE + DW_HALO, :]

    c = conv_ref[...]
    mu = jnp.mean(c, axis=-1, keepdims=True)
    cen = c - mu
    var = jnp.mean(cen * cen, axis=-1, keepdims=True)
    y = cen * lax.rsqrt(var + EPS) * lnw_ref[...] + lnb_ref[...]
    cc = _silu(y) * _silu(_unpack(zb_ref[0]).astype(F32))
    y_b = jnp.dot(cc.astype(BF16), _unpack(wcf_ref[...]), preferred_element_type=F32) + bcf_ref[...]
    y_a = jnp.dot(_unpack(og_ref[0]), _unpack(wdn_ref[...]), preferred_element_type=F32)
    merged = (jax.nn.sigmoid(_unpack(ga_ref[0]).astype(F32)) * y_a
              + jax.nn.sigmoid(_unpack(gb_ref[0]).astype(F32)) * y_b)
    xo = x_ref[0] + jnp.dot(merged.astype(BF16), _unpack(wo_ref[...]), preferred_element_type=F32)
    ms = jnp.mean(xo * xo, axis=-1, keepdims=True)
    out_ref[0] = xo * lax.rsqrt(ms + EPS) * fnw_ref[...]


def _tile_constants():
    i = np.arange(TILE)[:, None]
    j = np.arange(TILE)[None, :]
    same = (i // CHUNK) == (j // CHUNK)
    incl = (same & (i >= j)).astype(np.float32)
    tri = incl
    x = i ^ j
    lvls = []
    b = 1
    while b < CHUNK:
        lvls.append(((x >= b) & (x < 2 * b) & (i > j)).astype(np.float32))
        b *= 2
    neg = np.where(incl > 0, 0.0, -1e30).astype(np.float32)
    return tri, np.stack([neg, np.eye(TILE, dtype=np.float32)]), np.stack(lvls)


def kernel(x, meta, norm_w, w_in, conv_qkv_w, a_log, dt_bias, dn_norm_w, w_dn_out,
           dw_w, dw_b, ln_w, ln_b, w_cf_out, b_cf_out, w_o, final_norm_w):
    bsz, seq, d = x.shape
    assert d == D_MODEL and seq % TILE == 0 and w_in.shape[0] == 1
    nt = seq // TILE + 1
    lp = nt * TILE

    w_main, w_ba = _pack_w_in(w_in[0])
    meta_tile = jnp.pad(meta.astype(F32), ((TILE - N_META, 0), (0, 0)))
    row = lambda v: v.reshape(1, -1).astype(F32)
    alog_row = jnp.pad(row(a_log[0]), ((0, 0), (DN_HEADS, 128 - 2 * DN_HEADS)))
    dtb_row = jnp.pad(row(dt_bias[0]), ((0, 0), (DN_HEADS, 128 - 2 * DN_HEADS)))
    dww = jnp.pad(dw_w[0].astype(F32), ((0, 32 - CF_KERNEL), (0, 0)))
    tri, msk, lvls = _tile_constants()

    cparams = pltpu.CompilerParams(dimension_semantics=("arbitrary", "arbitrary"),
                                   vmem_limit_bytes=VMEM_LIMIT)
    xmap = lambda b, t: (b, jnp.maximum(t - 1, 0), 0)
    const2 = lambda b, t: (0, 0)
    const3 = lambda b, t: (0, 0, 0)

    proj, ba = pl.pallas_call(
        _proj_kernel,
        grid=(bsz, nt),
        in_specs=[
            pl.BlockSpec((1, TILE, D_MODEL), xmap),
            pl.BlockSpec((TILE, D_MODEL), const2),
            pl.BlockSpec((1, D_MODEL), const2),
            pl.BlockSpec((D_MODEL // 2, N_MAIN_BLK * BLK), const2, pipeline_mode=pl.Buffered(1)),
            pl.BlockSpec((D_MODEL // 2, 128), const2),
            pl.BlockSpec((DN_CONV, 3 * BLK), const2),
        ],
        out_specs=[
            pl.BlockSpec((1, HALF, N_MAIN_BLK * BLK), lambda b, t: (b, t, 0)),
            pl.BlockSpec((1, TILE, 128), lambda b, t: (b, t, 0)),
        ],
        out_shape=[
            jax.ShapeDtypeStruct((bsz, lp // 2, N_MAIN_BLK * BLK), U32),
            jax.ShapeDtypeStruct((bsz, lp, 128), F32),
        ],
        scratch_shapes=[pltpu.VMEM((TILE + 8, 3 * BLK), F32),
                        pltpu.VMEM((DN_CONV - 1, 3 * DN_HEADS, TILE, DN_DK), F32)],
        compiler_params=cparams,
        name="proj",
    )(x, meta_tile, row(norm_w[0]), w_main, w_ba, conv_qkv_w[0].astype(F32))

    col = lambda j: pl.BlockSpec((1, HALF, BLK), lambda b, t, j=j: (b, t, j))
    og = pl.pallas_call(
        _delta_kernel,
        grid=(bsz, nt),
        in_specs=[
            col(0), col(1), col(2), col(3),
            pl.BlockSpec((1, TILE, 128), lambda b, t: (b, t, 0)),
            pl.BlockSpec((1, 128), const2),
            pl.BlockSpec((1, 128), const2),
            pl.BlockSpec((1, DN_DV), const2),
            pl.BlockSpec((TILE, TILE), const2),
            pl.BlockSpec((2, TILE, TILE), const3),
            pl.BlockSpec((6, TILE, TILE), const3),
        ],
        out_specs=pl.BlockSpec((1, HALF, BLK), lambda b, t: (b, t, 0)),
        out_shape=jax.ShapeDtypeStruct((bsz, lp // 2, BLK), U32),
        scratch_shapes=[pltpu.VMEM((DN_HEADS, DN_DK, DN_DV), F32)],
        compiler_params=cparams,
        name="delta",
    )(proj, proj, proj, proj, ba, alog_row, dtb_row,
      row(dn_norm_w[0]), jnp.asarray(tri, BF16), jnp.asarray(msk), jnp.asarray(lvls, BF16))

    wspec = pl.BlockSpec((D_MODEL // 2, D_MODEL), const2)
    vspec = pl.BlockSpec((1, D_MODEL), const2)
    out = pl.pallas_call(
        _merge_kernel,
        grid=(bsz, nt),
        in_specs=[
            pl.BlockSpec((1, HALF, BLK), lambda b, t: (b, t, 0)),
            col(4), col(5), col(6), col(7), col(8),
            pl.BlockSpec((1, TILE, D_MODEL), xmap),
            wspec, wspec, wspec,
            pl.BlockSpec((32, D_MODEL), const2),
            vspec, vspec, vspec, vspec, vspec,
        ],
        out_specs=pl.BlockSpec((1, TILE, D_MODEL), xmap),
        out_shape=jax.ShapeDtypeStruct((bsz, seq, D_MODEL), F32),
        scratch_shapes=[
            pltpu.VMEM((DW_HALO + TILE, BLK), F32),
            pltpu.VMEM((TILE, BLK), F32),
            pltpu.VMEM((2, 7, SH_ROWS, 128), F32),
        ],
        compiler_params=cparams,
        name="merge",
    )(og, proj, proj, proj, proj, proj, x,
      _pack_weight(w_dn_out[0], "pack_w_dn"), _pack_weight(w_cf_out[0], "pack_w_cf"),
      _pack_weight(w_o[0], "pack_w_o"),
      dww, row(dw_b[0]), row(ln_w[0]), row(ln_b[0]), row(b_cf_out[0]), row(final_norm_w))
    return out
```

```python
import numpy as np
import jax
import jax.numpy as jnp
from jax import lax
from jax.experimental import pallas as pl
from jax.experimental.pallas import tpu as pltpu

D_MODEL = 1024
N_META = 16
EPS = 1e-6
DN_HEADS = 8
DN_DK = 128
DN_DV = 128
DN_CONV = 4
CHUNK = 64
CF_KERNEL = 31

TILE = 256
HALF = TILE // 2
CHUNKS_PER_TILE = TILE // CHUNK
BLK = 1024
N_MAIN_BLK = 9
N_PROJ_BLK = 8
SUB = 256
CONV_RB = 64
DW_HALO = 32
SH_ROWS = DW_HALO + TILE - 8
VMEM_LIMIT = 56 * 1024 * 1024

F32 = jnp.float32
BF16 = jnp.bfloat16
U32 = jnp.uint32


def _unpack(ref_slice):
    return pltpu.bitcast(ref_slice, BF16)


def _pack(x):
    return pltpu.bitcast(x.astype(BF16), U32)


def _pack_kernel(w_ref, o_ref):
    o_ref[...] = _pack(w_ref[0])


def _pack_weight(w, name):
    _, k, n = w.shape
    return pl.pallas_call(
        _pack_kernel,
        grid=(n // BLK,),
        in_specs=[pl.BlockSpec((1, k, BLK), lambda j: (0, 0, j))],
        out_specs=pl.BlockSpec((k // 2, BLK), lambda j: (0, j)),
        out_shape=jax.ShapeDtypeStruct((k // 2, n), U32),
        name=name,
    )(w.astype(F32))


def _pack_w_in_kernel(w_ref, main_ref, ba_ref):
    x = w_ref[0]
    qkvz = 4 * BLK
    nba = 2 * DN_HEADS
    main_ref[:, 0:qkvz] = _pack(x[:, 0:qkvz])
    main_ref[:, qkvz:] = _pack(x[:, qkvz + nba:])
    ba = jnp.concatenate([x[:, qkvz:qkvz + nba], jnp.zeros((x.shape[0], 128 - nba), F32)], axis=1)
    ba_ref[...] = _pack(ba)


def _pack_w_in(w):
    _, k, n = w.shape
    rb = 128
    return pl.pallas_call(
        _pack_w_in_kernel,
        grid=(k // rb,),
        in_specs=[pl.BlockSpec((1, rb, n), lambda i: (0, i, 0))],
        out_specs=[pl.BlockSpec((rb // 2, N_MAIN_BLK * BLK), lambda i: (i, 0)),
                   pl.BlockSpec((rb // 2, 128), lambda i: (i, 0))],
        out_shape=[jax.ShapeDtypeStruct((k // 2, N_MAIN_BLK * BLK), U32),
                   jax.ShapeDtypeStruct((k // 2, 128), U32)],
        name="pack_w_in",
    )(w.astype(F32))


def _mm_nt(a, b):
    return lax.dot_general(a.astype(BF16), b.astype(BF16), (((1,), (1,)), ((), ())),
                           preferred_element_type=F32)


def _mm_tn(a, b):
    return lax.dot_general(a.astype(BF16), b.astype(BF16), (((0,), (0,)), ((), ())),
                           preferred_element_type=F32)


def _silu(x):
    return x * jax.nn.sigmoid(x)


def _proj_kernel(x_ref, meta_ref, nw_ref, w_ref, wba_ref, cw_ref, proj_ref, ba_ref, pre_ref, tap_ref):
    t = pl.program_id(1)
    last = DN_CONV - 1

    @pl.when(t == 0)
    def _():
        pre_ref[TILE:TILE + 8, :] = jnp.zeros((8, 3 * BLK), F32)

    xin = jnp.where(t == 0, meta_ref[...], x_ref[0])
    ms = jnp.mean(xin * xin, axis=-1, keepdims=True)
    h = (xin * lax.rsqrt(ms + EPS) * nw_ref[...]).astype(BF16)

    pre_ref[0:8, :] = pre_ref[TILE:TILE + 8, :]
    for j in range(3):
        cs = slice(j * BLK, (j + 1) * BLK)
        pre_ref[8:8 + TILE, cs] = jnp.dot(h, _unpack(w_ref[:, cs]), preferred_element_type=F32)
        for tap in range(last):
            d = last - tap
            for hh in range(DN_HEADS):
                c0 = j * BLK + hh * DN_DK
                tap_ref[tap, j * DN_HEADS + hh] = pre_ref[8 - d:8 - d + TILE, c0:c0 + DN_DK]

    def conv_block(kind, hh):
        col0 = kind * BLK + hh * DN_DK
        cols = slice(col0, col0 + DN_DK)
        for r0 in range(0, TILE, CONV_RB):
            acc = cw_ref[last:last + 1, cols] * pre_ref[8 + r0:8 + r0 + CONV_RB, cols]
            for tap in range(last):
                acc = acc + cw_ref[tap:tap + 1, cols] * tap_ref[tap, kind * DN_HEADS + hh, r0:r0 + CONV_RB, :]
            y = _silu(acc)
            if kind == 0:
                y = y * lax.rsqrt(jnp.sum(y * y, axis=-1, keepdims=True) + EPS) * (DN_DK ** -0.5)
            elif kind == 1:
                y = y * lax.rsqrt(jnp.sum(y * y, axis=-1, keepdims=True) + EPS)
            proj_ref[0, r0 // 2:(r0 + CONV_RB) // 2, cols] = _pack(y)

    blocks = iter([(kind, hh) for kind in range(3) for hh in range(DN_HEADS)])

    def piece(wblk, p):
        cs = slice(wblk * BLK + p * SUB, wblk * BLK + (p + 1) * SUB)
        r = jnp.dot(h, _unpack(w_ref[:, cs]), preferred_element_type=F32)
        conv_block(*next(blocks))
        return r

    def out_cols(oblk, p):
        return slice(oblk * BLK + p * SUB, oblk * BLK + (p + 1) * SUB)

    for p in range(BLK // SUB):
        proj_ref[0, :, out_cols(3, p)] = _pack(piece(3, p))
    for p in range(BLK // SUB):
        proj_ref[0, :, out_cols(4, p)] = _pack(piece(4, p) * jax.nn.sigmoid(piece(5, p)))
    for p in range(BLK // SUB):
        proj_ref[0, :, out_cols(5, p)] = _pack(_silu(piece(6, p)))
    for p in range(BLK // SUB):
        proj_ref[0, :, out_cols(6, p)] = _pack(jax.nn.sigmoid(piece(7, p)))
    for p in range(BLK // SUB):
        proj_ref[0, :, out_cols(7, p)] = _pack(jax.nn.sigmoid(piece(8, p)))
    ba_ref[0] = jnp.dot(h, _unpack(wba_ref[...]), preferred_element_type=F32)


def _delta_kernel(q_ref, k_ref, v_ref, za_ref, ba_ref, alog_ref, dtb_ref, dnw_ref,
                  tri_ref, msk_ref, lvl_ref, og_ref, s_ref):
    t = pl.program_id(1)

    @pl.when(t == 0)
    def _():
        s_ref[...] = jnp.zeros_like(s_ref)

    ba = ba_ref[0]
    beta_all = jax.nn.sigmoid(ba)
    g_all = -jnp.exp(alog_ref[...]) * jax.nn.softplus(ba + dtb_ref[...])
    g_hi = g_all.astype(BF16)
    g_r1 = g_all - g_hi.astype(F32)
    g_mid = g_r1.astype(BF16)
    g_lo = (g_r1 - g_mid.astype(F32)).astype(BF16)
    tri = tri_ref[...]
    gc_all = (jnp.dot(tri, g_hi, preferred_element_type=F32)
              + jnp.dot(tri, g_mid, preferred_element_type=F32)
              + jnp.dot(tri, g_lo, preferred_element_type=F32))
    gl_all = jnp.concatenate(
        [jnp.broadcast_to(gc_all[(c + 1) * CHUNK - 1:(c + 1) * CHUNK, :], (CHUNK, 128))
         for c in range(CHUNKS_PER_TILE)], axis=0)
    gc_t = gc_all.T
    egc_all = jnp.exp(gc_all)
    eend_all = jnp.exp(gl_all - gc_all)
    egl_all = jnp.exp(gl_all)

    neg = msk_ref[0]
    eye_bf = msk_ref[1].astype(BF16)
    heads = range(DN_HEADS)

    nm, tv, attn, rhs, qg, kend, egl = [], [], [], [], [], [], []
    for h in heads:
        c0 = h * DN_DK
        qn = _unpack(q_ref[0, :, c0:c0 + DN_DK]).astype(F32)
        kn_bf = _unpack(k_ref[0, :, c0:c0 + DN_DK])
        kn = kn_bf.astype(F32)
        vs = _unpack(v_ref[0, :, c0:c0 + DN_DV]).astype(F32)

        c_beta = beta_all[:, h:h + 1]
        c_gc = gc_all[:, DN_HEADS + h:DN_HEADS + h + 1]
        r_gc = gc_t[DN_HEADS + h:DN_HEADS + h + 1, :]
        c_egc = egc_all[:, DN_HEADS + h:DN_HEADS + h + 1]
        c_eend = eend_all[:, DN_HEADS + h:DN_HEADS + h + 1]
        egl.append(egl_all[:, DN_HEADS + h:DN_HEADS + h + 1])

        kb = kn * c_beta
        decay = jnp.exp((c_gc - r_gc) + neg)
        nm_h = (_mm_nt(kb, kn_bf) * decay).astype(BF16)
        nm.append(nm_h)
        attn.append((_mm_nt(qn, kn_bf) * decay).astype(BF16))
        rhs.append(jnp.concatenate([vs * c_beta, kb * c_egc], axis=1).astype(BF16))
        qg.append((qn * c_egc).astype(BF16))
        kend.append((kn * c_eend).astype(BF16))
        tv.append(eye_bf - nm_h * lvl_ref[0])

    for lv in range(1, 6):
        x1 = [jnp.dot(nm[h] * lvl_ref[lv], tv[h], preferred_element_type=F32).astype(BF16)
              for h in heads]
        tv = [tv[h] - jnp.dot(tv[h], x1[h], preferred_element_type=F32).astype(BF16)
              for h in heads]

    u, w = [], []
    for h in heads:
        sol = jnp.dot(tv[h], rhs[h], preferred_element_type=F32)
        u.append(sol[:, 0:DN_DV])
        w.append(sol[:, DN_DV:DN_DV + DN_DK].astype(BF16))

    s = [s_ref[h] for h in heads]
    wv = [[] for _ in heads]
    o_s = [[] for _ in heads]
    for c in range(CHUNKS_PER_TILE):
        rs = slice(c * CHUNK, (c + 1) * CHUNK)
        for h in heads:
            s_bf = s[h].astype(BF16)
            ws = jnp.dot(jnp.concatenate([w[h][rs], qg[h][rs]], axis=0), s_bf,
                         preferred_element_type=F32)
            wv_c = u[h][rs] - ws[0:CHUNK]
            o_s[h].append(ws[CHUNK:2 * CHUNK])
            s[h] = s[h] * egl[h][c * CHUNK:c * CHUNK + 1, :] + _mm_tn(kend[h][rs], wv_c)
            wv[h].append(wv_c.astype(BF16))

    for h in heads:
        c0 = h * DN_DK
        s_ref[h] = s[h]
        o = jnp.concatenate(o_s[h], axis=0) + jnp.dot(attn[h], jnp.concatenate(wv[h], axis=0),
                                                      preferred_element_type=F32)
        on = o * lax.rsqrt(jnp.mean(o * o, axis=-1, keepdims=True) + EPS) * dnw_ref[...]
        za = _unpack(za_ref[0, :, c0:c0 + DN_DV]).astype(F32)
        og_ref[0, :, c0:c0 + DN_DV] = _pack(on * _silu(za))


def _merge_kernel(og_ref, glu_ref, szb_ref, sga_ref, sgb_ref, x_ref,
                  wdn_ref, wcf_ref, wo_ref, dww_ref, dwb_ref, lnw_ref, lnb_ref, bcf_ref, fnw_ref,
                  out_ref, cbuf_ref, conv_ref, sh_ref):
    t = pl.program_id(1)

    @pl.when(t == 0)
    def _():
        cbuf_ref[0:DW_HALO, :] = jnp.zeros((DW_HALO, BLK), F32)

    cbuf_ref[DW_HALO:DW_HALO + TILE, :] = _unpack(glu_ref[0]).astype(F32)

    rb = 64
    first = DW_HALO - (CF_KERNEL - 1)
    for cb in range(BLK // 128):
        cs = slice(cb * 128, (cb + 1) * 128)
        slot = cb % 2
        for s in range(1, 8):
            sh_ref[slot, s - 1] = cbuf_ref[s:s + SH_ROWS, cs]
        for r in range(TILE // rb):
            acc = jnp.broadcast_to(dwb_ref[:, cs], (rb, 128))
            for j in range(CF_KERNEL):
                s = (first + j) % 8
                r0 = (first + j) - s + r * rb
                if s == 0:
                    rows = cbuf_ref[r0:r0 + rb, cs]
                else:
                    rows = sh_ref[slot, s - 1, r0:r0 + rb, :]
                acc = acc + dww_ref[j:j + 1, cs] * rows
            conv_ref[r * rb:(r + 1) * rb, cs] = acc

    cbuf_ref[0:DW_HALO, :] = cbuf_ref[TILE:TILE + DW_HALO, :]

    c = conv_ref[...]
    mu = jnp.mean(c, axis=-1, keepdims=True)
    cen = c - mu
    var = jnp.mean(cen * cen, axis=-1, keepdims=True)
    y = cen * lax.rsqrt(var + EPS) * lnw_ref[...] + lnb_ref[...]
    cc = _silu(y) * _unpack(szb_ref[0]).astype(F32)
    y_b = jnp.dot(cc.astype(BF16), _unpack(wcf_ref[...]), preferred_element_type=F32) + bcf_ref[...]
    y_a = jnp.dot(_unpack(og_ref[0]), _unpack(wdn_ref[...]), preferred_element_type=F32)
    merged = _unpack(sga_ref[0]).astype(F32) * y_a + _unpack(sgb_ref[0]).astype(F32) * y_b
    xo = x_ref[0] + jnp.dot(merged.astype(BF16), _unpack(wo_ref[...]), preferred_element_type=F32)
    ms = jnp.mean(xo * xo, axis=-1, keepdims=True)
    out_ref[0] = xo * lax.rsqrt(ms + EPS) * fnw_ref[...]


def _tile_constants():
    i = np.arange(TILE)[:, None]
    j = np.arange(TILE)[None, :]
    same = (i // CHUNK) == (j // CHUNK)
    incl = (same & (i >= j)).astype(np.float32)
    tri = incl
    x = i ^ j
    lvls = []
    b = 1
    while b < CHUNK:
        lvls.append(((x >= b) & (x < 2 * b) & (i > j)).astype(np.float32))
        b *= 2
    neg = np.where(incl > 0, 0.0, -1e30).astype(np.float32)
    return tri, np.stack([neg, np.eye(TILE, dtype=np.float32)]), np.stack(lvls)


def kernel(x, meta, norm_w, w_in, conv_qkv_w, a_log, dt_bias, dn_norm_w, w_dn_out,
           dw_w, dw_b, ln_w, ln_b, w_cf_out, b_cf_out, w_o, final_norm_w):
    bsz, seq, d = x.shape
    assert d == D_MODEL and seq % TILE == 0 and w_in.shape[0] == 1
    nt = seq // TILE + 1
    lp = nt * TILE

    w_main, w_ba = _pack_w_in(w_in)
    meta_tile = jnp.pad(meta.astype(F32), ((TILE - N_META, 0), (0, 0)))
    row = lambda v: v.reshape(1, -1).astype(F32)
    alog_row = jnp.pad(row(a_log[0]), ((0, 0), (DN_HEADS, 128 - 2 * DN_HEADS)))
    dtb_row = jnp.pad(row(dt_bias[0]), ((0, 0), (DN_HEADS, 128 - 2 * DN_HEADS)))
    dww = jnp.pad(dw_w[0].astype(F32), ((0, 32 - CF_KERNEL), (0, 0)))
    tri, msk, lvls = _tile_constants()

    cparams = pltpu.CompilerParams(dimension_semantics=("arbitrary", "arbitrary"),
                                   vmem_limit_bytes=VMEM_LIMIT)
    xmap = lambda b, t: (b, jnp.maximum(t - 1, 0), 0)
    const2 = lambda b, t: (0, 0)
    const3 = lambda b, t: (0, 0, 0)

    proj, ba = pl.pallas_call(
        _proj_kernel,
        grid=(bsz, nt),
        in_specs=[
            pl.BlockSpec((1, TILE, D_MODEL), xmap),
            pl.BlockSpec((TILE, D_MODEL), const2),
            pl.BlockSpec((1, D_MODEL), const2),
            pl.BlockSpec((D_MODEL // 2, N_MAIN_BLK * BLK), const2, pipeline_mode=pl.Buffered(1)),
            pl.BlockSpec((D_MODEL // 2, 128), const2),
            pl.BlockSpec((DN_CONV, 3 * BLK), const2),
        ],
        out_specs=[
            pl.BlockSpec((1, HALF, N_PROJ_BLK * BLK), lambda b, t: (b, t, 0)),
            pl.BlockSpec((1, TILE, 128), lambda b, t: (b, t, 0)),
        ],
        out_shape=[
            jax.ShapeDtypeStruct((bsz, lp // 2, N_PROJ_BLK * BLK), U32),
            jax.ShapeDtypeStruct((bsz, lp, 128), F32),
        ],
        scratch_shapes=[pltpu.VMEM((TILE + 8, 3 * BLK), F32),
                        pltpu.VMEM((DN_CONV - 1, 3 * DN_HEADS, TILE, DN_DK), F32)],
        compiler_params=cparams,
        name="proj",
    )(x, meta_tile, row(norm_w[0]), w_main, w_ba, conv_qkv_w[0].astype(F32))

    col = lambda j: pl.BlockSpec((1, HALF, BLK), lambda b, t, j=j: (b, t, j))
    og = pl.pallas_call(
        _delta_kernel,
        grid=(bsz, nt),
        in_specs=[
            col(0), col(1), col(2), col(3),
            pl.BlockSpec((1, TILE, 128), lambda b, t: (b, t, 0)),
            pl.BlockSpec((1, 128), const2),
            pl.BlockSpec((1, 128), const2),
            pl.BlockSpec((1, DN_DV), const2),
            pl.BlockSpec((TILE, TILE), const2),
            pl.BlockSpec((2, TILE, TILE), const3),
            pl.BlockSpec((6, TILE, TILE), const3),
        ],
        out_specs=pl.BlockSpec((1, HALF, BLK), lambda b, t: (b, t, 0)),
        out_shape=jax.ShapeDtypeStruct((bsz, lp // 2, BLK), U32),
        scratch_shapes=[pltpu.VMEM((DN_HEADS, DN_DK, DN_DV), F32)],
        compiler_params=cparams,
        name="delta",
    )(proj, proj, proj, proj, ba, alog_row, dtb_row,
      row(dn_norm_w[0]), jnp.asarray(tri, BF16), jnp.asarray(msk), jnp.asarray(lvls, BF16))

    wspec = pl.BlockSpec((D_MODEL // 2, D_MODEL), const2)
    vspec = pl.BlockSpec((1, D_MODEL), const2)
    out = pl.pallas_call(
        _merge_kernel,
        grid=(bsz, nt),
        in_specs=[
            pl.BlockSpec((1, HALF, BLK), lambda b, t: (b, t, 0)),
            col(4), col(5), col(6), col(7),
            pl.BlockSpec((1, TILE, D_MODEL), xmap),
            wspec, wspec, wspec,
            pl.BlockSpec((32, D_MODEL), const2),
            vspec, vspec, vspec, vspec, vspec,
        ],
        out_specs=pl.BlockSpec((1, TILE, D_MODEL), xmap),
        out_shape=jax.ShapeDtypeStruct((bsz, seq, D_MODEL), F32),
        scratch_shapes=[
            pltpu.VMEM((DW_HALO + TILE, BLK), F32),
            pltpu.VMEM((TILE, BLK), F32),
            pltpu.VMEM((2, 7, SH_ROWS, 128), F32),
        ],
        compiler_params=cparams,
        name="merge",
    )(og, proj, proj, proj, proj, x,
      _pack_weight(w_dn_out, "pack_w_dn"), _pack_weight(w_cf_out, "pack_w_cf"),
      _pack_weight(w_o, "pack_w_o"),
      dww, row(dw_b[0]), row(ln_w[0]), row(ln_b[0]), row(b_cf_out[0]), row(final_norm_w))
    return out
```

```python
import numpy as np
import jax
import jax.numpy as jnp
from jax import lax
from jax.experimental import pallas as pl
from jax.experimental.pallas import tpu as pltpu

D_MODEL = 1024
N_META = 16
EPS = 1e-6
DN_HEADS = 8
DN_DK = 128
DN_DV = 128
DN_CONV = 4
CHUNK = 64
CF_KERNEL = 31

TILE = 256
HALF = TILE // 2
CHUNKS_PER_TILE = TILE // CHUNK
BLK = 1024
N_MAIN_BLK = 9
N_PROJ_BLK = 7
SUB = 256
CONV_RB = 64
DW_HALO = 32
SH_ROWS = DW_HALO + TILE - 8
VMEM_LIMIT = 56 * 1024 * 1024

F32 = jnp.float32
BF16 = jnp.bfloat16
U32 = jnp.uint32


def _unpack(ref_slice):
    return pltpu.bitcast(ref_slice, BF16)


def _pack(x):
    return pltpu.bitcast(x.astype(BF16), U32)


def _pack_kernel(w_ref, o_ref):
    o_ref[...] = _pack(w_ref[0])


def _pack_weight(w, name):
    _, k, n = w.shape
    return pl.pallas_call(
        _pack_kernel,
        grid=(n // BLK,),
        in_specs=[pl.BlockSpec((1, k, BLK), lambda j: (0, 0, j))],
        out_specs=pl.BlockSpec((k // 2, BLK), lambda j: (0, j)),
        out_shape=jax.ShapeDtypeStruct((k // 2, n), U32),
        name=name,
    )(w.astype(F32))


def _pack_w_in_kernel(w_ref, main_ref, ba_ref):
    x = w_ref[0]
    qkvz = 4 * BLK
    nba = 2 * DN_HEADS
    main_ref[:, 0:qkvz] = _pack(x[:, 0:qkvz])
    main_ref[:, qkvz:] = _pack(x[:, qkvz + nba:])
    ba = jnp.concatenate([x[:, qkvz:qkvz + nba], jnp.zeros((x.shape[0], 128 - nba), F32)], axis=1)
    ba_ref[...] = _pack(ba)


def _pack_w_in(w):
    _, k, n = w.shape
    rb = 128
    return pl.pallas_call(
        _pack_w_in_kernel,
        grid=(k // rb,),
        in_specs=[pl.BlockSpec((1, rb, n), lambda i: (0, i, 0))],
        out_specs=[pl.BlockSpec((rb // 2, N_MAIN_BLK * BLK), lambda i: (i, 0)),
                   pl.BlockSpec((rb // 2, 128), lambda i: (i, 0))],
        out_shape=[jax.ShapeDtypeStruct((k // 2, N_MAIN_BLK * BLK), U32),
                   jax.ShapeDtypeStruct((k // 2, 128), U32)],
        name="pack_w_in",
    )(w.astype(F32))


def _mm_nt(a, b):
    return lax.dot_general(a.astype(BF16), b.astype(BF16), (((1,), (1,)), ((), ())),
                           preferred_element_type=F32)


def _mm_tn(a, b):
    return lax.dot_general(a.astype(BF16), b.astype(BF16), (((0,), (0,)), ((), ())),
                           preferred_element_type=F32)


def _silu(x):
    return x * jax.nn.sigmoid(x)


def _proj_kernel(x_ref, meta_ref, nw_ref, w_ref, wba_ref, cw_ref, dww_ref, dwb_ref, lnw_ref, lnb_ref,
                 proj_ref, ba_ref, pre_ref, tap_ref, cbuf_ref, conv_ref, sh_ref, szb_ref):
    t = pl.program_id(1)
    last = DN_CONV - 1
    n_sub = BLK // SUB

    @pl.when(t == 0)
    def _():
        pre_ref[TILE:TILE + 8, :] = jnp.zeros((8, 3 * BLK), F32)
        cbuf_ref[0:DW_HALO, :] = jnp.zeros((DW_HALO, BLK), F32)

    xin = jnp.where(t == 0, meta_ref[...], x_ref[0])
    ms = jnp.mean(xin * xin, axis=-1, keepdims=True)
    h = (xin * lax.rsqrt(ms + EPS) * nw_ref[...]).astype(BF16)

    def dot_piece(wblk, p):
        cs = slice(wblk * BLK + p * SUB, wblk * BLK + (p + 1) * SUB)
        return jnp.dot(h, _unpack(w_ref[:, cs]), preferred_element_type=F32)

    def out_cols(oblk, p):
        return slice(oblk * BLK + p * SUB, oblk * BLK + (p + 1) * SUB)

    def conv_block(kind, hh):
        col0 = kind * BLK + hh * DN_DK
        cols = slice(col0, col0 + DN_DK)
        for r0 in range(0, TILE, CONV_RB):
            acc = cw_ref[last:last + 1, cols] * pre_ref[8 + r0:8 + r0 + CONV_RB, cols]
            for tap in range(last):
                acc = acc + cw_ref[tap:tap + 1, cols] * tap_ref[tap, kind * DN_HEADS + hh, r0:r0 + CONV_RB, :]
            y = _silu(acc)
            if kind == 0:
                y = y * lax.rsqrt(jnp.sum(y * y, axis=-1, keepdims=True) + EPS) * (DN_DK ** -0.5)
            elif kind == 1:
                y = y * lax.rsqrt(jnp.sum(y * y, axis=-1, keepdims=True) + EPS)
            proj_ref[0, r0 // 2:(r0 + CONV_RB) // 2, cols] = _pack(y)

    conv_units = [lambda kind=kind, hh=hh: conv_block(kind, hh)
                  for kind in range(3) for hh in range(DN_HEADS)]

    rb = 64
    first = DW_HALO - (CF_KERNEL - 1)

    def dw_unit(cb, r):
        cs = slice(cb * 128, (cb + 1) * 128)
        slot = cb % 2
        if r == 0:
            for s in range(1, 8):
                sh_ref[slot, s - 1] = cbuf_ref[s:s + SH_ROWS, cs]
        acc = jnp.broadcast_to(dwb_ref[:, cs], (rb, 128))
        for j in range(CF_KERNEL):
            s = (first + j) % 8
            r0 = (first + j) - s + r * rb
            if s == 0:
                rows = cbuf_ref[r0:r0 + rb, cs]
            else:
                rows = sh_ref[slot, s - 1, r0:r0 + rb, :]
            acc = acc + dww_ref[j:j + 1, cs] * rows
        conv_ref[r * rb:(r + 1) * rb, cs] = acc

    dw_units = [lambda cb=cb, r=r: dw_unit(cb, r) for cb in range(BLK // 128) for r in range(TILE // rb)]

    def run(units, n):
        for _ in range(min(n, len(units))):
            units.pop(0)()

    for p in range(n_sub):
        cbuf_ref[DW_HALO:DW_HALO + TILE, p * SUB:(p + 1) * SUB] = (
            dot_piece(4, p) * jax.nn.sigmoid(dot_piece(5, p)))

    pre_ref[0:8, :] = pre_ref[TILE:TILE + 8, :]
    for j in range(3):
        for p in range(n_sub):
            pre_ref[8:8 + TILE, out_cols(j, p)] = dot_piece(j, p)
            for tap in range(last):
                d = last - tap
                for hh in range(p * SUB // DN_DK, (p + 1) * SUB // DN_DK):
                    c0 = j * BLK + hh * DN_DK
                    tap_ref[tap, j * DN_HEADS + hh] = pre_ref[8 - d:8 - d + TILE, c0:c0 + DN_DK]
            run(dw_units, 1)

    for p in range(n_sub):
        proj_ref[0, :, out_cols(3, p)] = _pack(dot_piece(3, p))
        run(conv_units, 2)
        run(dw_units, 1)
    for p in range(n_sub):
        szb_ref[:, p * SUB:(p + 1) * SUB] = _silu(dot_piece(6, p))
        run(conv_units, 2)
        run(dw_units, 1)
    for p in range(n_sub):
        proj_ref[0, :, out_cols(5, p)] = _pack(jax.nn.sigmoid(dot_piece(7, p)))
        run(conv_units, 1)
        run(dw_units, 1)
    for p in range(n_sub):
        proj_ref[0, :, out_cols(6, p)] = _pack(jax.nn.sigmoid(dot_piece(8, p)))
        run(conv_units, 1)
        run(dw_units, 1)
    ba_ref[0] = jnp.dot(h, _unpack(wba_ref[...]), preferred_element_type=F32)
    run(conv_units, len(conv_units))
    run(dw_units, len(dw_units))

    cbuf_ref[0:DW_HALO, :] = cbuf_ref[TILE:TILE + DW_HALO, :]
    c = conv_ref[...]
    mu = jnp.mean(c, axis=-1, keepdims=True)
    cen = c - mu
    var = jnp.mean(cen * cen, axis=-1, keepdims=True)
    y = cen * lax.rsqrt(var + EPS) * lnw_ref[...] + lnb_ref[...]
    proj_ref[0, :, 4 * BLK:5 * BLK] = _pack(_silu(y) * szb_ref[...])


def _delta_kernel(q_ref, k_ref, v_ref, za_ref, ba_ref, alog_ref, dtb_ref, dnw_ref,
                  tri_ref, msk_ref, lvl_ref, og_ref, s_ref):
    t = pl.program_id(1)

    @pl.when(t == 0)
    def _():
        s_ref[...] = jnp.zeros_like(s_ref)

    ba = ba_ref[0]
    beta_all = jax.nn.sigmoid(ba)
    g_all = -jnp.exp(alog_ref[...]) * jax.nn.softplus(ba + dtb_ref[...])
    g_hi = g_all.astype(BF16)
    g_r1 = g_all - g_hi.astype(F32)
    g_mid = g_r1.astype(BF16)
    g_lo = (g_r1 - g_mid.astype(F32)).astype(BF16)
    tri = tri_ref[...]
    gc_all = (jnp.dot(tri, g_hi, preferred_element_type=F32)
              + jnp.dot(tri, g_mid, preferred_element_type=F32)
              + jnp.dot(tri, g_lo, preferred_element_type=F32))
    gl_all = jnp.concatenate(
        [jnp.broadcast_to(gc_all[(c + 1) * CHUNK - 1:(c + 1) * CHUNK, :], (CHUNK, 128))
         for c in range(CHUNKS_PER_TILE)], axis=0)
    gc_t = gc_all.T
    egc_all = jnp.exp(gc_all)
    eend_all = jnp.exp(gl_all - gc_all)
    egl_all = jnp.exp(gl_all)

    neg = msk_ref[0]
    eye_bf = msk_ref[1].astype(BF16)
    heads = range(DN_HEADS)

    nm, tv, attn, rhs, qg, kend, egl = [], [], [], [], [], [], []
    for h in heads:
        c0 = h * DN_DK
        qn = _unpack(q_ref[0, :, c0:c0 + DN_DK]).astype(F32)
        kn_bf = _unpack(k_ref[0, :, c0:c0 + DN_DK])
        kn = kn_bf.astype(F32)
        vs = _unpack(v_ref[0, :, c0:c0 + DN_DV]).astype(F32)

        c_beta = beta_all[:, h:h + 1]
        c_gc = gc_all[:, DN_HEADS + h:DN_HEADS + h + 1]
        r_gc = gc_t[DN_HEADS + h:DN_HEADS + h + 1, :]
        c_egc = egc_all[:, DN_HEADS + h:DN_HEADS + h + 1]
        c_eend = eend_all[:, DN_HEADS + h:DN_HEADS + h + 1]
        egl.append(egl_all[:, DN_HEADS + h:DN_HEADS + h + 1])

        kb = kn * c_beta
        decay = jnp.exp((c_gc - r_gc) + neg)
        nm_h = (_mm_nt(kb, kn_bf) * decay).astype(BF16)
        nm.append(nm_h)
        attn.append((_mm_nt(qn, kn_bf) * decay).astype(BF16))
        rhs.append(jnp.concatenate([vs * c_beta, kb * c_egc], axis=1).astype(BF16))
        qg.append((qn * c_egc).astype(BF16))
        kend.append((kn * c_eend).astype(BF16))
        tv.append(eye_bf - nm_h * lvl_ref[0])

    for lv in range(1, 6):
        x1 = [jnp.dot(nm[h] * lvl_ref[lv], tv[h], preferred_element_type=F32).astype(BF16)
              for h in heads]
        tv = [tv[h] - jnp.dot(tv[h], x1[h], preferred_element_type=F32).astype(BF16)
              for h in heads]

    u, w = [], []
    for h in heads:
        sol = jnp.dot(tv[h], rhs[h], preferred_element_type=F32)
        u.append(sol[:, 0:DN_DV])
        w.append(sol[:, DN_DV:DN_DV + DN_DK].astype(BF16))

    s = [s_ref[h] for h in heads]
    wv = [[] for _ in heads]
    o_s = [[] for _ in heads]
    for c in range(CHUNKS_PER_TILE):
        rs = slice(c * CHUNK, (c + 1) * CHUNK)
        for h in heads:
            s_bf = s[h].astype(BF16)
            ws = jnp.dot(jnp.concatenate([w[h][rs], qg[h][rs]], axis=0), s_bf,
                         preferred_element_type=F32)
            wv_c = u[h][rs] - ws[0:CHUNK]
            o_s[h].append(ws[CHUNK:2 * CHUNK])
            s[h] = s[h] * egl[h][c * CHUNK:c * CHUNK + 1, :] + _mm_tn(kend[h][rs], wv_c)
            wv[h].append(wv_c.astype(BF16))

    for h in heads:
        c0 = h * DN_DK
        s_ref[h] = s[h]
        o = jnp.concatenate(o_s[h], axis=0) + jnp.dot(attn[h], jnp.concatenate(wv[h], axis=0),
                                                      preferred_element_type=F32)
        on = o * lax.rsqrt(jnp.mean(o * o, axis=-1, keepdims=True) + EPS) * dnw_ref[...]
        za = _unpack(za_ref[0, :, c0:c0 + DN_DV]).astype(F32)
        og_ref[0, :, c0:c0 + DN_DV] = _pack(on * _silu(za))


def _merge_kernel(og_ref, cc_ref, sga_ref, sgb_ref, x_ref, wdn_ref, wcf_ref, wo_ref, bcf_ref, fnw_ref,
                  out_ref):
    y_b = jnp.dot(_unpack(cc_ref[0]), _unpack(wcf_ref[...]), preferred_element_type=F32) + bcf_ref[...]
    y_a = jnp.dot(_unpack(og_ref[0]), _unpack(wdn_ref[...]), preferred_element_type=F32)
    merged = _unpack(sga_ref[0]).astype(F32) * y_a + _unpack(sgb_ref[0]).astype(F32) * y_b
    xo = x_ref[0] + jnp.dot(merged.astype(BF16), _unpack(wo_ref[...]), preferred_element_type=F32)
    ms = jnp.mean(xo * xo, axis=-1, keepdims=True)
    out_ref[0] = xo * lax.rsqrt(ms + EPS) * fnw_ref[...]


def _tile_constants():
    i = np.arange(TILE)[:, None]
    j = np.arange(TILE)[None, :]
    same = (i // CHUNK) == (j // CHUNK)
    incl = (same & (i >= j)).astype(np.float32)
    tri = incl
    x = i ^ j
    lvls = []
    b = 1
    while b < CHUNK:
        lvls.append(((x >= b) & (x < 2 * b) & (i > j)).astype(np.float32))
        b *= 2
    neg = np.where(incl > 0, 0.0, -1e30).astype(np.float32)
    return tri, np.stack([neg, np.eye(TILE, dtype=np.float32)]), np.stack(lvls)


def kernel(x, meta, norm_w, w_in, conv_qkv_w, a_log, dt_bias, dn_norm_w, w_dn_out,
           dw_w, dw_b, ln_w, ln_b, w_cf_out, b_cf_out, w_o, final_norm_w):
    bsz, seq, d = x.shape
    assert d == D_MODEL and seq % TILE == 0 and w_in.shape[0] == 1
    nt = seq // TILE + 1
    lp = nt * TILE

    w_main, w_ba = _pack_w_in(w_in)
    meta_tile = jnp.pad(meta.astype(F32), ((TILE - N_META, 0), (0, 0)))
    row = lambda v: v.reshape(1, -1).astype(F32)
    alog_row = jnp.pad(row(a_log[0]), ((0, 0), (DN_HEADS, 128 - 2 * DN_HEADS)))
    dtb_row = jnp.pad(row(dt_bias[0]), ((0, 0), (DN_HEADS, 128 - 2 * DN_HEADS)))
    dww = jnp.pad(dw_w[0].astype(F32), ((0, 32 - CF_KERNEL), (0, 0)))
    tri, msk, lvls = _tile_constants()

    cparams = pltpu.CompilerParams(dimension_semantics=("arbitrary", "arbitrary"),
                                   vmem_limit_bytes=VMEM_LIMIT)
    xmap = lambda b, t: (b, jnp.maximum(t - 1, 0), 0)
    const2 = lambda b, t: (0, 0)
    const3 = lambda b, t: (0, 0, 0)
    vspec = pl.BlockSpec((1, D_MODEL), const2)

    proj, ba = pl.pallas_call(
        _proj_kernel,
        grid=(bsz, nt),
        in_specs=[
            pl.BlockSpec((1, TILE, D_MODEL), xmap),
            pl.BlockSpec((TILE, D_MODEL), const2),
            pl.BlockSpec((1, D_MODEL), const2),
            pl.BlockSpec((D_MODEL // 2, N_MAIN_BLK * BLK), const2, pipeline_mode=pl.Buffered(1)),
            pl.BlockSpec((D_MODEL // 2, 128), const2),
            pl.BlockSpec((DN_CONV, 3 * BLK), const2),
            pl.BlockSpec((32, D_MODEL), const2),
            vspec, vspec, vspec,
        ],
        out_specs=[
            pl.BlockSpec((1, HALF, N_PROJ_BLK * BLK), lambda b, t: (b, t, 0)),
            pl.BlockSpec((1, TILE, 128), lambda b, t: (b, t, 0)),
        ],
        out_shape=[
            jax.ShapeDtypeStruct((bsz, lp // 2, N_PROJ_BLK * BLK), U32),
            jax.ShapeDtypeStruct((bsz, lp, 128), F32),
        ],
        scratch_shapes=[pltpu.VMEM((TILE + 8, 3 * BLK), F32),
                        pltpu.VMEM((DN_CONV - 1, 3 * DN_HEADS, TILE, DN_DK), F32),
                        pltpu.VMEM((DW_HALO + TILE, BLK), F32),
                        pltpu.VMEM((TILE, BLK), F32),
                        pltpu.VMEM((2, 7, SH_ROWS, 128), F32),
                        pltpu.VMEM((TILE, BLK), F32)],
        compiler_params=cparams,
        name="proj",
    )(x, meta_tile, row(norm_w[0]), w_main, w_ba, conv_qkv_w[0].astype(F32),
      dww, row(dw_b[0]), row(ln_w[0]), row(ln_b[0]))

    col = lambda j: pl.BlockSpec((1, HALF, BLK), lambda b, t, j=j: (b, t, j))
    og = pl.pallas_call(
        _delta_kernel,
        grid=(bsz, nt),
        in_specs=[
            col(0), col(1), col(2), col(3),
            pl.BlockSpec((1, TILE, 128), lambda b, t: (b, t, 0)),
            pl.BlockSpec((1, 128), const2),
            pl.BlockSpec((1, 128), const2),
            pl.BlockSpec((1, DN_DV), const2),
            pl.BlockSpec((TILE, TILE), const2),
            pl.BlockSpec((2, TILE, TILE), const3),
            pl.BlockSpec((6, TILE, TILE), const3),
        ],
        out_specs=pl.BlockSpec((1, HALF, BLK), lambda b, t: (b, t, 0)),
        out_shape=jax.ShapeDtypeStruct((bsz, lp // 2, BLK), U32),
        scratch_shapes=[pltpu.VMEM((DN_HEADS, DN_DK, DN_DV), F32)],
        compiler_params=cparams,
        name="delta",
    )(proj, proj, proj, proj, ba, alog_row, dtb_row,
      row(dn_norm_w[0]), jnp.asarray(tri, BF16), jnp.asarray(msk), jnp.asarray(lvls, BF16))

    wspec = pl.BlockSpec((D_MODEL // 2, D_MODEL), const2)
    out = pl.pallas_call(
        _merge_kernel,
        grid=(bsz, nt),
        in_specs=[
            pl.BlockSpec((1, HALF, BLK), lambda b, t: (b, t, 0)),
            col(4), col(5), col(6),
            pl.BlockSpec((1, TILE, D_MODEL), xmap),
            wspec, wspec, wspec,
            vspec, vspec,
        ],
        out_specs=pl.BlockSpec((1, TILE, D_MODEL), xmap),
        out_shape=jax.ShapeDtypeStruct((bsz, seq, D_MODEL), F32),
        compiler_params=cparams,
        name="merge",
    )(og, proj, proj, proj, x,
      _pack_weight(w_dn_out, "pack_w_dn"), _pack_weight(w_cf_out, "pack_w_cf"),
      _pack_weight(w_o, "pack_w_o"),
      row(b_cf_out[0]), row(final_norm_w))
    return out
```

```python
import numpy as np
import jax
import jax.numpy as jnp
from jax import lax
from jax.experimental import pallas as pl
from jax.experimental.pallas import tpu as pltpu

D_MODEL = 1024
N_META = 16
EPS = 1e-6
DN_HEADS = 8
DN_DK = 128
DN_DV = 128
DN_CONV = 4
CHUNK = 64
CF_KERNEL = 31

TILE = 256
HALF = TILE // 2
CHUNKS_PER_TILE = TILE // CHUNK
BLK = 1024
N_MAIN_BLK = 9
N_PROJ_BLK = 7
SUB = 256
CONV_RB = 64
GROUP = 4
GROUP_LAG = 1
DW_HALO = 32
SH_ROWS = DW_HALO + TILE - 8
VMEM_LIMIT = 56 * 1024 * 1024

F32 = jnp.float32
BF16 = jnp.bfloat16
U32 = jnp.uint32


def _unpack(ref_slice):
    return pltpu.bitcast(ref_slice, BF16)


def _pack(x):
    return pltpu.bitcast(x.astype(BF16), U32)


def _pack_kernel(w_ref, o_ref):
    o_ref[...] = _pack(w_ref[0])


def _pack_weight(w, name):
    _, k, n = w.shape
    return pl.pallas_call(
        _pack_kernel,
        grid=(n // BLK,),
        in_specs=[pl.BlockSpec((1, k, BLK), lambda j: (0, 0, j))],
        out_specs=pl.BlockSpec((k // 2, BLK), lambda j: (0, j)),
        out_shape=jax.ShapeDtypeStruct((k // 2, n), U32),
        name=name,
    )(w.astype(F32))


def _pack_w_in_kernel(w_ref, main_ref, ba_ref):
    x = w_ref[0]
    qkvz = 4 * BLK
    nba = 2 * DN_HEADS
    main_ref[:, 0:qkvz] = _pack(x[:, 0:qkvz])
    main_ref[:, qkvz:] = _pack(x[:, qkvz + nba:])
    ba = jnp.concatenate([x[:, qkvz:qkvz + nba], jnp.zeros((x.shape[0], 128 - nba), F32)], axis=1)
    ba_ref[...] = _pack(ba)


def _pack_w_in(w):
    _, k, n = w.shape
    rb = 128
    return pl.pallas_call(
        _pack_w_in_kernel,
        grid=(k // rb,),
        in_specs=[pl.BlockSpec((1, rb, n), lambda i: (0, i, 0))],
        out_specs=[pl.BlockSpec((rb // 2, N_MAIN_BLK * BLK), lambda i: (i, 0)),
                   pl.BlockSpec((rb // 2, 128), lambda i: (i, 0))],
        out_shape=[jax.ShapeDtypeStruct((k // 2, N_MAIN_BLK * BLK), U32),
                   jax.ShapeDtypeStruct((k // 2, 128), U32)],
        name="pack_w_in",
    )(w.astype(F32))


def _mm_nt(a, b):
    return lax.dot_general(a.astype(BF16), b.astype(BF16), (((1,), (1,)), ((), ())),
                           preferred_element_type=F32)


def _mm_tn(a, b):
    return lax.dot_general(a.astype(BF16), b.astype(BF16), (((0,), (0,)), ((), ())),
                           preferred_element_type=F32)


def _silu(x):
    return x * jax.nn.sigmoid(x)


def _proj_kernel(x_ref, meta_ref, nw_ref, w_ref, wba_ref, cw_ref, dww_ref, dwb_ref, lnw_ref, lnb_ref,
                 proj_ref, ba_ref, pre_ref, tap_ref, cbuf_ref, conv_ref, sh_ref, szb_ref):
    t = pl.program_id(1)
    last = DN_CONV - 1
    n_sub = BLK // SUB

    @pl.when(t == 0)
    def _():
        pre_ref[TILE:TILE + 8, :] = jnp.zeros((8, 3 * BLK), F32)
        cbuf_ref[0:DW_HALO, :] = jnp.zeros((DW_HALO, BLK), F32)

    xin = jnp.where(t == 0, meta_ref[...], x_ref[0])
    ms = jnp.mean(xin * xin, axis=-1, keepdims=True)
    h = (xin * lax.rsqrt(ms + EPS) * nw_ref[...]).astype(BF16)

    def dot_piece(wblk, p):
        cs = slice(wblk * BLK + p * SUB, wblk * BLK + (p + 1) * SUB)
        return jnp.dot(h, _unpack(w_ref[:, cs]), preferred_element_type=F32)

    def out_cols(oblk, p):
        return slice(oblk * BLK + p * SUB, oblk * BLK + (p + 1) * SUB)

    def conv_block(kind, hh):
        col0 = kind * BLK + hh * DN_DK
        cols = slice(col0, col0 + DN_DK)
        for r0 in range(0, TILE, CONV_RB):
            acc = cw_ref[last:last + 1, cols] * pre_ref[8 + r0:8 + r0 + CONV_RB, cols]
            for tap in range(last):
                acc = acc + cw_ref[tap:tap + 1, cols] * tap_ref[tap, kind * DN_HEADS + hh, r0:r0 + CONV_RB, :]
            y = _silu(acc)
            if kind == 0:
                y = y * lax.rsqrt(jnp.sum(y * y, axis=-1, keepdims=True) + EPS) * (DN_DK ** -0.5)
            elif kind == 1:
                y = y * lax.rsqrt(jnp.sum(y * y, axis=-1, keepdims=True) + EPS)
            proj_ref[0, r0 // 2:(r0 + CONV_RB) // 2, cols] = _pack(y)

    conv_units = [lambda kind=kind, hh=hh: conv_block(kind, hh)
                  for kind in range(3) for hh in range(DN_HEADS)]

    rb = 64
    first = DW_HALO - (CF_KERNEL - 1)

    def dw_unit(cb, r):
        cs = slice(cb * 128, (cb + 1) * 128)
        slot = cb % 2
        if r == 0:
            for s in range(1, 8):
                sh_ref[slot, s - 1] = cbuf_ref[s:s + SH_ROWS, cs]
        acc = jnp.broadcast_to(dwb_ref[:, cs], (rb, 128))
        for j in range(CF_KERNEL):
            s = (first + j) % 8
            r0 = (first + j) - s + r * rb
            if s == 0:
                rows = cbuf_ref[r0:r0 + rb, cs]
            else:
                rows = sh_ref[slot, s - 1, r0:r0 + rb, :]
            acc = acc + dww_ref[j:j + 1, cs] * rows
        conv_ref[r * rb:(r + 1) * rb, cs] = acc

    dw_units = [lambda cb=cb, r=r: dw_unit(cb, r) for cb in range(BLK // 128) for r in range(TILE // rb)]

    def run(units, n):
        for _ in range(min(n, len(units))):
            units.pop(0)()

    for p in range(n_sub):
        cbuf_ref[DW_HALO:DW_HALO + TILE, p * SUB:(p + 1) * SUB] = (
            dot_piece(4, p) * jax.nn.sigmoid(dot_piece(5, p)))

    pre_ref[0:8, :] = pre_ref[TILE:TILE + 8, :]
    for j in range(3):
        for p in range(n_sub):
            pre_ref[8:8 + TILE, out_cols(j, p)] = dot_piece(j, p)
            for tap in range(last):
                d = last - tap
                for hh in range(p * SUB // DN_DK, (p + 1) * SUB // DN_DK):
                    c0 = j * BLK + hh * DN_DK
                    tap_ref[tap, j * DN_HEADS + hh] = pre_ref[8 - d:8 - d + TILE, c0:c0 + DN_DK]
            run(dw_units, 1)

    for p in range(n_sub):
        proj_ref[0, :, out_cols(3, p)] = _pack(dot_piece(3, p))
        run(conv_units, 2)
        run(dw_units, 1)
    for p in range(n_sub):
        szb_ref[:, p * SUB:(p + 1) * SUB] = _silu(dot_piece(6, p))
        run(conv_units, 2)
        run(dw_units, 1)
    for p in range(n_sub):
        proj_ref[0, :, out_cols(5, p)] = _pack(jax.nn.sigmoid(dot_piece(7, p)))
        run(conv_units, 1)
        run(dw_units, 1)
    for p in range(n_sub):
        proj_ref[0, :, out_cols(6, p)] = _pack(jax.nn.sigmoid(dot_piece(8, p)))
        run(conv_units, 1)
        run(dw_units, 1)
    ba_ref[0] = jnp.dot(h, _unpack(wba_ref[...]), preferred_element_type=F32)
    run(conv_units, len(conv_units))
    run(dw_units, len(dw_units))

    cbuf_ref[0:DW_HALO, :] = cbuf_ref[TILE:TILE + DW_HALO, :]
    c = conv_ref[...]
    mu = jnp.mean(c, axis=-1, keepdims=True)
    cen = c - mu
    var = jnp.mean(cen * cen, axis=-1, keepdims=True)
    y = cen * lax.rsqrt(var + EPS) * lnw_ref[...] + lnb_ref[...]
    proj_ref[0, :, 4 * BLK:5 * BLK] = _pack(_silu(y) * szb_ref[...])


def _delta_kernel(q_ref, k_ref, v_ref, za_ref, ba_ref, alog_ref, dtb_ref, dnw_ref,
                  tri_ref, msk_ref, lvl_ref, og_ref, s_ref):
    t = pl.program_id(1)

    @pl.when(t == 0)
    def _():
        s_ref[...] = jnp.zeros_like(s_ref)

    ba = ba_ref[0]
    beta_all = jax.nn.sigmoid(ba)
    g_all = -jnp.exp(alog_ref[...]) * jax.nn.softplus(ba + dtb_ref[...])
    g_hi = g_all.astype(BF16)
    g_r1 = g_all - g_hi.astype(F32)
    g_mid = g_r1.astype(BF16)
    g_lo = (g_r1 - g_mid.astype(F32)).astype(BF16)
    tri = tri_ref[...]
    gc_all = (jnp.dot(tri, g_hi, preferred_element_type=F32)
              + jnp.dot(tri, g_mid, preferred_element_type=F32)
              + jnp.dot(tri, g_lo, preferred_element_type=F32))
    gl_all = jnp.concatenate(
        [jnp.broadcast_to(gc_all[(c + 1) * CHUNK - 1:(c + 1) * CHUNK, :], (CHUNK, 128))
         for c in range(CHUNKS_PER_TILE)], axis=0)
    gc_t = gc_all.T
    egc_all = jnp.exp(gc_all)
    eend_all = jnp.exp(gl_all - gc_all)
    egl_all = jnp.exp(gl_all)

    neg = msk_ref[0]
    eye_bf = msk_ref[1].astype(BF16)
    heads = range(DN_HEADS)

    st = [dict() for _ in heads]

    def prepare(h):
        d = st[h]
        c0 = h * DN_DK
        qn = _unpack(q_ref[0, :, c0:c0 + DN_DK]).astype(F32)
        kn_bf = _unpack(k_ref[0, :, c0:c0 + DN_DK])
        kn = kn_bf.astype(F32)
        vs = _unpack(v_ref[0, :, c0:c0 + DN_DV]).astype(F32)

        c_beta = beta_all[:, h:h + 1]
        c_gc = gc_all[:, DN_HEADS + h:DN_HEADS + h + 1]
        r_gc = gc_t[DN_HEADS + h:DN_HEADS + h + 1, :]
        c_egc = egc_all[:, DN_HEADS + h:DN_HEADS + h + 1]
        c_eend = eend_all[:, DN_HEADS + h:DN_HEADS + h + 1]
        d['egl'] = egl_all[:, DN_HEADS + h:DN_HEADS + h + 1]

        kb = kn * c_beta
        decay = jnp.exp((c_gc - r_gc) + neg)
        d['nm'] = (_mm_nt(kb, kn_bf) * decay).astype(BF16)
        d['attn'] = (_mm_nt(qn, kn_bf) * decay).astype(BF16)
        d['rhs'] = jnp.concatenate([vs * c_beta, kb * c_egc], axis=1).astype(BF16)
        d['qg'] = (qn * c_egc).astype(BF16)
        d['kend'] = (kn * c_eend).astype(BF16)
        d['tv'] = eye_bf - d['nm'] * lvl_ref[0]

    def level(h, lv):
        d = st[h]
        x1 = jnp.dot(d['nm'] * lvl_ref[lv], d['tv'], preferred_element_type=F32).astype(BF16)
        d['tv'] = d['tv'] - jnp.dot(d['tv'], x1, preferred_element_type=F32).astype(BF16)

    def solve(h):
        d = st[h]
        sol = jnp.dot(d['tv'], d['rhs'], preferred_element_type=F32)
        d['u'] = sol[:, 0:DN_DV]
        d['w'] = sol[:, DN_DV:DN_DV + DN_DK].astype(BF16)
        d['s'] = s_ref[h]
        d['wv'] = []
        d['o_s'] = []

    def chunk(h, c):
        d = st[h]
        rs = slice(c * CHUNK, (c + 1) * CHUNK)
        ws = jnp.dot(jnp.concatenate([d['w'][rs], d['qg'][rs]], axis=0), d['s'].astype(BF16),
                     preferred_element_type=F32)
        wv_c = d['u'][rs] - ws[0:CHUNK]
        d['o_s'].append(ws[CHUNK:2 * CHUNK])
        d['s'] = d['s'] * d['egl'][c * CHUNK:c * CHUNK + 1, :] + _mm_tn(d['kend'][rs], wv_c)
        d['wv'].append(wv_c.astype(BF16))

    def finish(h):
        d = st[h]
        c0 = h * DN_DK
        s_ref[h] = d['s']
        o = jnp.concatenate(d['o_s'], axis=0) + jnp.dot(d['attn'], jnp.concatenate(d['wv'], axis=0),
                                                        preferred_element_type=F32)
        on = o * lax.rsqrt(jnp.mean(o * o, axis=-1, keepdims=True) + EPS) * dnw_ref[...]
        za = _unpack(za_ref[0, :, c0:c0 + DN_DV]).astype(F32)
        og_ref[0, :, c0:c0 + DN_DV] = _pack(on * _silu(za))

    stages = ([prepare] + [lambda h, lv=lv: level(h, lv) for lv in range(1, 6)] + [solve]
              + [lambda h, c=c: chunk(h, c) for c in range(CHUNKS_PER_TILE)] + [finish])

    lag = [GROUP_LAG * (h // GROUP) for h in heads]
    for k in range(len(stages) + max(lag)):
        for h in heads:
            i = k - lag[h]
            if 0 <= i < len(stages):
                stages[i](h)


def _merge_kernel(og_ref, cc_ref, sga_ref, sgb_ref, x_ref, wdn_ref, wcf_ref, wo_ref, bcf_ref, fnw_ref,
                  out_ref):
    y_b = jnp.dot(_unpack(cc_ref[0]), _unpack(wcf_ref[...]), preferred_element_type=F32) + bcf_ref[...]
    y_a = jnp.dot(_unpack(og_ref[0]), _unpack(wdn_ref[...]), preferred_element_type=F32)
    merged = _unpack(sga_ref[0]).astype(F32) * y_a + _unpack(sgb_ref[0]).astype(F32) * y_b
    xo = x_ref[0] + jnp.dot(merged.astype(BF16), _unpack(wo_ref[...]), preferred_element_type=F32)
    ms = jnp.mean(xo * xo, axis=-1, keepdims=True)
    out_ref[0] = xo * lax.rsqrt(ms + EPS) * fnw_ref[...]


def _tile_constants():
    i = np.arange(TILE)[:, None]
    j = np.arange(TILE)[None, :]
    same = (i // CHUNK) == (j // CHUNK)
    incl = (same & (i >= j)).astype(np.float32)
    tri = incl
    x = i ^ j
    lvls = []
    b = 1
    while b < CHUNK:
        lvls.append(((x >= b) & (x < 2 * b) & (i > j)).astype(np.float32))
        b *= 2
    neg = np.where(incl > 0, 0.0, -1e30).astype(np.float32)
    return tri, np.stack([neg, np.eye(TILE, dtype=np.float32)]), np.stack(lvls)


def kernel(x, meta, norm_w, w_in, conv_qkv_w, a_log, dt_bias, dn_norm_w, w_dn_out,
           dw_w, dw_b, ln_w, ln_b, w_cf_out, b_cf_out, w_o, final_norm_w):
    bsz, seq, d = x.shape
    assert d == D_MODEL and seq % TILE == 0 and w_in.shape[0] == 1
    nt = seq // TILE + 1
    lp = nt * TILE

    w_main, w_ba = _pack_w_in(w_in)
    meta_tile = jnp.pad(meta.astype(F32), ((TILE - N_META, 0), (0, 0)))
    row = lambda v: v.reshape(1, -1).astype(F32)
    alog_row = jnp.pad(row(a_log[0]), ((0, 0), (DN_HEADS, 128 - 2 * DN_HEADS)))
    dtb_row = jnp.pad(row(dt_bias[0]), ((0, 0), (DN_HEADS, 128 - 2 * DN_HEADS)))
    dww = jnp.pad(dw_w[0].astype(F32), ((0, 32 - CF_KERNEL), (0, 0)))
    tri, msk, lvls = _tile_constants()

    cparams = pltpu.CompilerParams(dimension_semantics=("arbitrary", "arbitrary"),
                                   vmem_limit_bytes=VMEM_LIMIT)
    xmap = lambda b, t: (b, jnp.maximum(t - 1, 0), 0)
    const2 = lambda b, t: (0, 0)
    const3 = lambda b, t: (0, 0, 0)
    vspec = pl.BlockSpec((1, D_MODEL), const2)

    proj, ba = pl.pallas_call(
        _proj_kernel,
        grid=(bsz, nt),
        in_specs=[
            pl.BlockSpec((1, TILE, D_MODEL), xmap),
            pl.BlockSpec((TILE, D_MODEL), const2),
            pl.BlockSpec((1, D_MODEL), const2),
            pl.BlockSpec((D_MODEL // 2, N_MAIN_BLK * BLK), const2, pipeline_mode=pl.Buffered(1)),
            pl.BlockSpec((D_MODEL // 2, 128), const2),
            pl.BlockSpec((DN_CONV, 3 * BLK), const2),
            pl.BlockSpec((32, D_MODEL), const2),
            vspec, vspec, vspec,
        ],
        out_specs=[
            pl.BlockSpec((1, HALF, N_PROJ_BLK * BLK), lambda b, t: (b, t, 0)),
            pl.BlockSpec((1, TILE, 128), lambda b, t: (b, t, 0)),
        ],
        out_shape=[
            jax.ShapeDtypeStruct((bsz, lp // 2, N_PROJ_BLK * BLK), U32),
            jax.ShapeDtypeStruct((bsz, lp, 128), F32),
        ],
        scratch_shapes=[pltpu.VMEM((TILE + 8, 3 * BLK), F32),
                        pltpu.VMEM((DN_CONV - 1, 3 * DN_HEADS, TILE, DN_DK), F32),
                        pltpu.VMEM((DW_HALO + TILE, BLK), F32),
                        pltpu.VMEM((TILE, BLK), F32),
                        pltpu.VMEM((2, 7, SH_ROWS, 128), F32),
                        pltpu.VMEM((TILE, BLK), F32)],
        compiler_params=cparams,
        name="proj",
    )(x, meta_tile, row(norm_w[0]), w_main, w_ba, conv_qkv_w[0].astype(F32),
      dww, row(dw_b[0]), row(ln_w[0]), row(ln_b[0]))

    col = lambda j: pl.BlockSpec((1, HALF, BLK), lambda b, t, j=j: (b, t, j))
    og = pl.pallas_call(
        _delta_kernel,
        grid=(bsz, nt),
        in_specs=[
            col(0), col(1), col(2), col(3),
            pl.BlockSpec((1, TILE, 128), lambda b, t: (b, t, 0)),
            pl.BlockSpec((1, 128), const2),
            pl.BlockSpec((1, 128), const2),
            pl.BlockSpec((1, DN_DV), const2),
            pl.BlockSpec((TILE, TILE), const2),
            pl.BlockSpec((2, TILE, TILE), const3),
            pl.BlockSpec((6, TILE, TILE), const3),
        ],
        out_specs=pl.BlockSpec((1, HALF, BLK), lambda b, t: (b, t, 0)),
        out_shape=jax.ShapeDtypeStruct((bsz, lp // 2, BLK), U32),
        scratch_shapes=[pltpu.VMEM((DN_HEADS, DN_DK, DN_DV), F32)],
        compiler_params=cparams,
        name="delta",
    )(proj, proj, proj, proj, ba, alog_row, dtb_row,
      row(dn_norm_w[0]), jnp.asarray(tri, BF16), jnp.asarray(msk), jnp.asarray(lvls, BF16))

    wspec = pl.BlockSpec((D_MODEL // 2, D_MODEL), const2)
    out = pl.pallas_call(
        _merge_kernel,
        grid=(bsz, nt),
        in_specs=[
            pl.BlockSpec((1, HALF, BLK), lambda b, t: (b, t, 0)),
            col(4), col(5), col(6),
            pl.BlockSpec((1, TILE, D_MODEL), xmap),
            wspec, wspec, wspec,
            vspec, vspec,
        ],
        out_specs=pl.BlockSpec((1, TILE, D_MODEL), xmap),
        out_shape=jax.ShapeDtypeStruct((bsz, seq, D_MODEL), F32),
        compiler_params=cparams,
        name="merge",
    )(og, proj, proj, proj, x,
      _pack_weight(w_dn_out, "pack_w_dn"), _pack_weight(w_cf_out, "pack_w_cf"),
      _pack_weight(w_o, "pack_w_o"),
      row(b_cf_out[0]), row(final_norm_w))
    return out
```

```python
import numpy as np
import jax
import jax.numpy as jnp
from jax import lax
from jax.experimental import pallas as pl
from jax.experimental.pallas import tpu as pltpu

D_MODEL = 1024
N_META = 16
EPS = 1e-6
DN_HEADS = 8
DN_DK = 128
DN_DV = 128
DN_CONV = 4
CHUNK = 64
CF_KERNEL = 31

TILE = 256
HALF = TILE // 2
CHUNKS_PER_TILE = TILE // CHUNK
BLK = 1024
N_MAIN_BLK = 9
N_PROJ_BLK = 7
SUB = 256
CONV_RB = 64
GROUP = 4
GROUP_LAG = 1
DW_HALO = 32
SH_ROWS = DW_HALO + TILE - 8
VMEM_LIMIT = 56 * 1024 * 1024

F32 = jnp.float32
BF16 = jnp.bfloat16
U32 = jnp.uint32


def _unpack(ref_slice):
    return pltpu.bitcast(ref_slice, BF16)


def _pack(x):
    return pltpu.bitcast(x.astype(BF16), U32)


def _pack_kernel(w_ref, o_ref):
    o_ref[...] = _pack(w_ref[0])


def _pack_weight(w, name):
    _, k, n = w.shape
    return pl.pallas_call(
        _pack_kernel,
        grid=(n // BLK,),
        in_specs=[pl.BlockSpec((1, k, BLK), lambda j: (0, 0, j))],
        out_specs=pl.BlockSpec((k // 2, BLK), lambda j: (0, j)),
        out_shape=jax.ShapeDtypeStruct((k // 2, n), U32),
        name=name,
    )(w.astype(F32))


def _pack_w_in_kernel(w_ref, main_ref, ba_ref):
    x = w_ref[0]
    qkvz = 4 * BLK
    nba = 2 * DN_HEADS
    main_ref[:, 0:qkvz] = _pack(x[:, 0:qkvz])
    main_ref[:, qkvz:] = _pack(x[:, qkvz + nba:])
    ba = jnp.concatenate([x[:, qkvz:qkvz + nba], jnp.zeros((x.shape[0], 128 - nba), F32)], axis=1)
    ba_ref[...] = _pack(ba)


def _pack_w_in(w):
    _, k, n = w.shape
    rb = 128
    return pl.pallas_call(
        _pack_w_in_kernel,
        grid=(k // rb,),
        in_specs=[pl.BlockSpec((1, rb, n), lambda i: (0, i, 0))],
        out_specs=[pl.BlockSpec((rb // 2, N_MAIN_BLK * BLK), lambda i: (i, 0)),
                   pl.BlockSpec((rb // 2, 128), lambda i: (i, 0))],
        out_shape=[jax.ShapeDtypeStruct((k // 2, N_MAIN_BLK * BLK), U32),
                   jax.ShapeDtypeStruct((k // 2, 128), U32)],
        name="pack_w_in",
    )(w.astype(F32))


def _mm_nt(a, b):
    return lax.dot_general(a.astype(BF16), b.astype(BF16), (((1,), (1,)), ((), ())),
                           preferred_element_type=F32)


def _mm_tn(a, b):
    return lax.dot_general(a.astype(BF16), b.astype(BF16), (((0,), (0,)), ((), ())),
                           preferred_element_type=F32)


def _silu(x):
    return x * jax.nn.sigmoid(x)


def _proj_kernel(x_ref, meta_ref, nw_ref, w_ref, wba_ref, cw_ref, dww_ref, dwb_ref, lnw_ref, lnb_ref,
                 proj_ref, ba_ref, pre_ref, tap_ref, cbuf_ref, conv_ref, sh_ref, szb_ref):
    t = pl.program_id(1)
    last = DN_CONV - 1
    n_sub = BLK // SUB

    @pl.when(t == 0)
    def _():
        pre_ref[TILE:TILE + 8, :] = jnp.zeros((8, 3 * BLK), F32)
        cbuf_ref[0:DW_HALO, :] = jnp.zeros((DW_HALO, BLK), F32)

    xin = jnp.where(t == 0, meta_ref[...], x_ref[0])
    ms = jnp.mean(xin * xin, axis=-1, keepdims=True)
    h = (xin * lax.rsqrt(ms + EPS) * nw_ref[...]).astype(BF16)

    def dot_piece(wblk, p):
        cs = slice(wblk * BLK + p * SUB, wblk * BLK + (p + 1) * SUB)
        return jnp.dot(h, _unpack(w_ref[:, cs]), preferred_element_type=F32)

    def out_cols(oblk, p):
        return slice(oblk * BLK + p * SUB, oblk * BLK + (p + 1) * SUB)

    def conv_block(kind, hh):
        col0 = kind * BLK + hh * DN_DK
        cols = slice(col0, col0 + DN_DK)
        for r0 in range(0, TILE, CONV_RB):
            acc = cw_ref[last:last + 1, cols] * pre_ref[8 + r0:8 + r0 + CONV_RB, cols]
            for tap in range(last):
                acc = acc + cw_ref[tap:tap + 1, cols] * tap_ref[tap, kind * DN_HEADS + hh, r0:r0 + CONV_RB, :]
            y = _silu(acc)
            if kind == 0:
                y = y * lax.rsqrt(jnp.sum(y * y, axis=-1, keepdims=True) + EPS) * (DN_DK ** -0.5)
            elif kind == 1:
                y = y * lax.rsqrt(jnp.sum(y * y, axis=-1, keepdims=True) + EPS)
            proj_ref[0, r0 // 2:(r0 + CONV_RB) // 2, cols] = _pack(y)

    heads_per_piece = SUB // DN_DK
    conv_units = [lambda kind=kind, hh=hh: conv_block(kind, hh)
                  for p in range(n_sub) for kind in range(3)
                  for hh in range(p * heads_per_piece, (p + 1) * heads_per_piece)]

    rb = 64
    first = DW_HALO - (CF_KERNEL - 1)

    def dw_unit(cb, r):
        cs = slice(cb * 128, (cb + 1) * 128)
        slot = cb % 2
        if r == 0:
            for s in range(1, 8):
                sh_ref[slot, s - 1] = cbuf_ref[s:s + SH_ROWS, cs]
        acc = jnp.broadcast_to(dwb_ref[:, cs], (rb, 128))
        for j in range(CF_KERNEL):
            s = (first + j) % 8
            r0 = (first + j) - s + r * rb
            if s == 0:
                rows = cbuf_ref[r0:r0 + rb, cs]
            else:
                rows = sh_ref[slot, s - 1, r0:r0 + rb, :]
            acc = acc + dww_ref[j:j + 1, cs] * rows
        conv_ref[r * rb:(r + 1) * rb, cs] = acc

    dw_units = [lambda cb=cb, r=r: dw_unit(cb, r) for cb in range(BLK // 128) for r in range(TILE // rb)]

    def run(units, n):
        for _ in range(min(n, len(units))):
            units.pop(0)()

    pre_ref[0:8, :] = pre_ref[TILE:TILE + 8, :]
    for p in range(n_sub):
        cbuf_ref[DW_HALO:DW_HALO + TILE, p * SUB:(p + 1) * SUB] = (
            dot_piece(4, p) * jax.nn.sigmoid(dot_piece(5, p)))
        for j in range(3):
            pre_ref[8:8 + TILE, out_cols(j, p)] = dot_piece(j, p)
            for tap in range(last):
                d = last - tap
                for hh in range(p * heads_per_piece, (p + 1) * heads_per_piece):
                    c0 = j * BLK + hh * DN_DK
                    tap_ref[tap, j * DN_HEADS + hh] = pre_ref[8 - d:8 - d + TILE, c0:c0 + DN_DK]
            run(dw_units, 2)
        proj_ref[0, :, out_cols(3, p)] = _pack(dot_piece(3, p))
        run(conv_units, 2)
        run(dw_units, 1)
        szb_ref[:, p * SUB:(p + 1) * SUB] = _silu(dot_piece(6, p))
        run(conv_units, 2)
        run(dw_units, 1)
        proj_ref[0, :, out_cols(5, p)] = _pack(jax.nn.sigmoid(dot_piece(7, p)))
        run(conv_units, 1)
        proj_ref[0, :, out_cols(6, p)] = _pack(jax.nn.sigmoid(dot_piece(8, p)))
        run(conv_units, 1)
    ba_ref[0] = jnp.dot(h, _unpack(wba_ref[...]), preferred_element_type=F32)
    run(conv_units, len(conv_units))
    run(dw_units, len(dw_units))

    cbuf_ref[0:DW_HALO, :] = cbuf_ref[TILE:TILE + DW_HALO, :]
    c = conv_ref[...]
    mu = jnp.mean(c, axis=-1, keepdims=True)
    cen = c - mu
    var = jnp.mean(cen * cen, axis=-1, keepdims=True)
    y = cen * lax.rsqrt(var + EPS) * lnw_ref[...] + lnb_ref[...]
    proj_ref[0, :, 4 * BLK:5 * BLK] = _pack(_silu(y) * szb_ref[...])


def _delta_kernel(q_ref, k_ref, v_ref, za_ref, ba_ref, alog_ref, dtb_ref, dnw_ref,
                  tri_ref, msk_ref, lvl_ref, og_ref, s_ref):
    t = pl.program_id(1)

    @pl.when(t == 0)
    def _():
        s_ref[...] = jnp.zeros_like(s_ref)

    ba = ba_ref[0]
    beta_all = jax.nn.sigmoid(ba)
    g_all = -jnp.exp(alog_ref[...]) * jax.nn.softplus(ba + dtb_ref[...])
    g_hi = g_all.astype(BF16)
    g_r1 = g_all - g_hi.astype(F32)
    g_mid = g_r1.astype(BF16)
    g_lo = (g_r1 - g_mid.astype(F32)).astype(BF16)
    tri = tri_ref[...]
    gc_all = (jnp.dot(tri, g_hi, preferred_element_type=F32)
              + jnp.dot(tri, g_mid, preferred_element_type=F32)
              + jnp.dot(tri, g_lo, preferred_element_type=F32))
    gl_all = jnp.concatenate(
        [jnp.broadcast_to(gc_all[(c + 1) * CHUNK - 1:(c + 1) * CHUNK, :], (CHUNK, 128))
         for c in range(CHUNKS_PER_TILE)], axis=0)
    gc_t = gc_all.T
    egc_all = jnp.exp(gc_all)
    eend_all = jnp.exp(gl_all - gc_all)
    egl_all = jnp.exp(gl_all)

    neg = msk_ref[0]
    eye_bf = msk_ref[1].astype(BF16)
    heads = range(DN_HEADS)

    st = [dict() for _ in heads]

    def prepare(h):
        d = st[h]
        c0 = h * DN_DK
        qn = _unpack(q_ref[0, :, c0:c0 + DN_DK]).astype(F32)
        kn_bf = _unpack(k_ref[0, :, c0:c0 + DN_DK])
        kn = kn_bf.astype(F32)
        vs = _unpack(v_ref[0, :, c0:c0 + DN_DV]).astype(F32)

        c_beta = beta_all[:, h:h + 1]
        c_gc = gc_all[:, DN_HEADS + h:DN_HEADS + h + 1]
        r_gc = gc_t[DN_HEADS + h:DN_HEADS + h + 1, :]
        c_egc = egc_all[:, DN_HEADS + h:DN_HEADS + h + 1]
        c_eend = eend_all[:, DN_HEADS + h:DN_HEADS + h + 1]
        d['egl'] = egl_all[:, DN_HEADS + h:DN_HEADS + h + 1]

        kb = kn * c_beta
        decay = jnp.exp((c_gc - r_gc) + neg)
        d['nm'] = (_mm_nt(kb, kn_bf) * decay).astype(BF16)
        d['attn'] = (_mm_nt(qn, kn_bf) * decay).astype(BF16)
        d['rhs'] = jnp.concatenate([vs * c_beta, kb * c_egc], axis=1).astype(BF16)
        d['qg'] = (qn * c_egc).astype(BF16)
        d['kend'] = (kn * c_eend).astype(BF16)
        d['tv'] = eye_bf - d['nm'] * lvl_ref[0]

    def level(h, lv):
        d = st[h]
        x1 = jnp.dot(d['nm'] * lvl_ref[lv], d['tv'], preferred_element_type=F32).astype(BF16)
        d['tv'] = d['tv'] - jnp.dot(d['tv'], x1, preferred_element_type=F32).astype(BF16)

    def solve(h):
        d = st[h]
        sol = jnp.dot(d['tv'], d['rhs'], preferred_element_type=F32)
        d['u'] = sol[:, 0:DN_DV]
        d['w'] = sol[:, DN_DV:DN_DV + DN_DK].astype(BF16)
        d['s'] = s_ref[h]
        d['wv'] = []
        d['o_s'] = []

    def chunk(h, c):
        d = st[h]
        rs = slice(c * CHUNK, (c + 1) * CHUNK)
        ws = jnp.dot(jnp.concatenate([d['w'][rs], d['qg'][rs]], axis=0), d['s'].astype(BF16),
                     preferred_element_type=F32)
        wv_c = d['u'][rs] - ws[0:CHUNK]
        d['o_s'].append(ws[CHUNK:2 * CHUNK])
        d['s'] = d['s'] * d['egl'][c * CHUNK:c * CHUNK + 1, :] + _mm_tn(d['kend'][rs], wv_c)
        d['wv'].append(wv_c.astype(BF16))

    def finish(h):
        d = st[h]
        c0 = h * DN_DK
        s_ref[h] = d['s']
        o = jnp.concatenate(d['o_s'], axis=0) + jnp.dot(d['attn'], jnp.concatenate(d['wv'], axis=0),
                                                        preferred_element_type=F32)
        on = o * lax.rsqrt(jnp.mean(o * o, axis=-1, keepdims=True) + EPS) * dnw_ref[...]
        za = _unpack(za_ref[0, :, c0:c0 + DN_DV]).astype(F32)
        og_ref[0, :, c0:c0 + DN_DV] = _pack(on * _silu(za))

    stages = ([prepare] + [lambda h, lv=lv: level(h, lv) for lv in range(1, 6)] + [solve]
              + [lambda h, c=c: chunk(h, c) for c in range(CHUNKS_PER_TILE)] + [finish])

    lag = [GROUP_LAG * (h // GROUP) for h in heads]
    for k in range(len(stages) + max(lag)):
        for h in heads:
            i = k - lag[h]
            if 0 <= i < len(stages):
                stages[i](h)


def _merge_kernel(og_ref, cc_ref, sga_ref, sgb_ref, x_ref, wdn_ref, wcf_ref, wo_ref, bcf_ref, fnw_ref,
                  out_ref):
    y_b = jnp.dot(_unpack(cc_ref[0]), _unpack(wcf_ref[...]), preferred_element_type=F32) + bcf_ref[...]
    y_a = jnp.dot(_unpack(og_ref[0]), _unpack(wdn_ref[...]), preferred_element_type=F32)
    merged = _unpack(sga_ref[0]).astype(F32) * y_a + _unpack(sgb_ref[0]).astype(F32) * y_b
    xo = x_ref[0] + jnp.dot(merged.astype(BF16), _unpack(wo_ref[...]), preferred_element_type=F32)
    ms = jnp.mean(xo * xo, axis=-1, keepdims=True)
    out_ref[0] = xo * lax.rsqrt(ms + EPS) * fnw_ref[...]


def _tile_constants():
    i = np.arange(TILE)[:, None]
    j = np.arange(TILE)[None, :]
    same = (i // CHUNK) == (j // CHUNK)
    incl = (same & (i >= j)).astype(np.float32)
    tri = incl
    x = i ^ j
    lvls = []
    b = 1
    while b < CHUNK:
        lvls.append(((x >= b) & (x < 2 * b) & (i > j)).astype(np.float32))
        b *= 2
    neg = np.where(incl > 0, 0.0, -1e30).astype(np.float32)
    return tri, np.stack([neg, np.eye(TILE, dtype=np.float32)]), np.stack(lvls)


def kernel(x, meta, norm_w, w_in, conv_qkv_w, a_log, dt_bias, dn_norm_w, w_dn_out,
           dw_w, dw_b, ln_w, ln_b, w_cf_out, b_cf_out, w_o, final_norm_w):
    bsz, seq, d = x.shape
    assert d == D_MODEL and seq % TILE == 0 and w_in.shape[0] == 1
    nt = seq // TILE + 1
    lp = nt * TILE

    w_main, w_ba = _pack_w_in(w_in)
    meta_tile = jnp.pad(meta.astype(F32), ((TILE - N_META, 0), (0, 0)))
    row = lambda v: v.reshape(1, -1).astype(F32)
    alog_row = jnp.pad(row(a_log[0]), ((0, 0), (DN_HEADS, 128 - 2 * DN_HEADS)))
    dtb_row = jnp.pad(row(dt_bias[0]), ((0, 0), (DN_HEADS, 128 - 2 * DN_HEADS)))
    dww = jnp.pad(dw_w[0].astype(F32), ((0, 32 - CF_KERNEL), (0, 0)))
    tri, msk, lvls = _tile_constants()

    cparams = pltpu.CompilerParams(dimension_semantics=("arbitrary", "arbitrary"),
                                   vmem_limit_bytes=VMEM_LIMIT)
    xmap = lambda b, t: (b, jnp.maximum(t - 1, 0), 0)
    const2 = lambda b, t: (0, 0)
    const3 = lambda b, t: (0, 0, 0)
    vspec = pl.BlockSpec((1, D_MODEL), const2)

    proj, ba = pl.pallas_call(
        _proj_kernel,
        grid=(bsz, nt),
        in_specs=[
            pl.BlockSpec((1, TILE, D_MODEL), xmap),
            pl.BlockSpec((TILE, D_MODEL), const2),
            pl.BlockSpec((1, D_MODEL), const2),
            pl.BlockSpec((D_MODEL // 2, N_MAIN_BLK * BLK), const2, pipeline_mode=pl.Buffered(1)),
            pl.BlockSpec((D_MODEL // 2, 128), const2),
            pl.BlockSpec((DN_CONV, 3 * BLK), const2),
            pl.BlockSpec((32, D_MODEL), const2),
            vspec, vspec, vspec,
        ],
        out_specs=[
            pl.BlockSpec((1, HALF, N_PROJ_BLK * BLK), lambda b, t: (b, t, 0)),
            pl.BlockSpec((1, TILE, 128), lambda b, t: (b, t, 0)),
        ],
        out_shape=[
            jax.ShapeDtypeStruct((bsz, lp // 2, N_PROJ_BLK * BLK), U32),
            jax.ShapeDtypeStruct((bsz, lp, 128), F32),
        ],
        scratch_shapes=[pltpu.VMEM((TILE + 8, 3 * BLK), F32),
                        pltpu.VMEM((DN_CONV - 1, 3 * DN_HEADS, TILE, DN_DK), F32),
                        pltpu.VMEM((DW_HALO + TILE, BLK), F32),
                        pltpu.VMEM((TILE, BLK), F32),
                        pltpu.VMEM((2, 7, SH_ROWS, 128), F32),
                        pltpu.VMEM((TILE, BLK), F32)],
        compiler_params=cparams,
        name="proj",
    )(x, meta_tile, row(norm_w[0]), w_main, w_ba, conv_qkv_w[0].astype(F32),
      dww, row(dw_b[0]), row(ln_w[0]), row(ln_b[0]))

    col = lambda j: pl.BlockSpec((1, HALF, BLK), lambda b, t, j=j: (b, t, j))
    og = pl.pallas_call(
        _delta_kernel,
        grid=(bsz, nt),
        in_specs=[
            col(0), col(1), col(2), col(3),
            pl.BlockSpec((1, TILE, 128), lambda b, t: (b, t, 0)),
            pl.BlockSpec((1, 128), const2),
            pl.BlockSpec((1, 128), const2),
            pl.BlockSpec((1, DN_DV), const2),
            pl.BlockSpec((TILE, TILE), const2),
            pl.BlockSpec((2, TILE, TILE), const3),
            pl.BlockSpec((6, TILE, TILE), const3),
        ],
        out_specs=pl.BlockSpec((1, HALF, BLK), lambda b, t: (b, t, 0)),
        out_shape=jax.ShapeDtypeStruct((bsz, lp // 2, BLK), U32),
        scratch_shapes=[pltpu.VMEM((DN_HEADS, DN_DK, DN_DV), F32)],
        compiler_params=cparams,
        name="delta",
    )(proj, proj, proj, proj, ba, alog_row, dtb_row,
      row(dn_norm_w[0]), jnp.asarray(tri, BF16), jnp.asarray(msk), jnp.asarray(lvls, BF16))

    wspec = pl.BlockSpec((D_MODEL // 2, D_MODEL), const2)
    out = pl.pallas_call(
        _merge_kernel,
        grid=(bsz, nt),
        in_specs=[
            pl.BlockSpec((1, HALF, BLK), lambda b, t: (b, t, 0)),
            col(4), col(5), col(6),
            pl.BlockSpec((1, TILE, D_MODEL), xmap),
            wspec, wspec, wspec,
            vspec, vspec,
        ],
        out_specs=pl.BlockSpec((1, TILE, D_MODEL), xmap),
        out_shape=jax.ShapeDtypeStruct((bsz, seq, D_MODEL), F32),
        compiler_params=cparams,
        name="merge",
    )(og, proj, proj, proj, x,
      _pack_weight(w_dn_out, "pack_w_dn"), _pack_weight(w_cf_out, "pack_w_cf"),
      _pack_weight(w_o, "pack_w_o"),
      row(b_cf_out[0]), row(final_norm_w))
    return out
```
